```python
import jax, jax.numpy as jnp
from jax import lax
import numpy as np

D_MODEL = 2048
BATCH = 4
SEQ = 2048
DEPTH = 2
DEC_BATCH = 32
DEC_SEQ = 64
PAST_LEN = 2048

CHUNK = 64
Q_BLOCK = 128
SB_HEAD_DIM = 128
SB_HEADS = D_MODEL // 256
SB_WIDTH = SB_HEADS * SB_HEAD_DIM
SGU_GROUP_DIM = 128
SGU_GROUPS = D_MODEL // 256
SGU_WIDTH = SGU_GROUPS * SGU_GROUP_DIM
SGU_LEN = 128
D_FF = -(-8 * D_MODEL // (3 * 256)) * 256
IN_COLS = 3 * SB_WIDTH + 2 * SGU_WIDTH + 2 * D_MODEL
EPS = 1e-6

kernel_name = "stickbreak_sgu_hybrid_stream_step"


def rmsnorm(x, g):
    x32 = x.astype(jnp.float32)
    y = x32 * lax.rsqrt(jnp.mean(x32 * x32, axis=-1, keepdims=True) + EPS)
    return (y * g.astype(jnp.float32)).astype(x.dtype)


def stick_breaking_block(q_blk, q_pos, k, v, k_pos):
    z = jnp.einsum('bqhd,bkhd->bhqk', q_blk.astype(jnp.float32), k.astype(jnp.float32)) * (SB_HEAD_DIM ** -0.5)
    mask = k_pos[None, :] < q_pos[:, None]
    log_beta = jax.nn.log_sigmoid(z)
    log_stay = jnp.where(mask, jax.nn.log_sigmoid(-z), 0.0)
    log_rest = lax.cumsum(log_stay, axis=3, reverse=True) - log_stay
    w = jnp.where(mask, jnp.exp(log_beta + log_rest), 0.0)
    o = jnp.einsum('bhqk,bkhd->bqhd', w, v.astype(jnp.float32))
    return o.astype(q_blk.dtype)


def stick_breaking_prompt(q, k, v):
    B, S = q.shape[0], q.shape[1]
    nb = S // Q_BLOCK
    pos = jnp.arange(S)
    qb = q.reshape(B, nb, Q_BLOCK, SB_HEADS, SB_HEAD_DIM).transpose(1, 0, 2, 3, 4)
    pb = pos.reshape(nb, Q_BLOCK)
    out = lax.map(lambda a: stick_breaking_block(a[0], a[1], k, v, pos), (qb, pb))
    return out.transpose(1, 0, 2, 3, 4).reshape(B, S, SB_HEADS, SB_HEAD_DIM)


def stick_breaking_sample(q, k_new, v_new, cache_k_l, cache_v_l):
    past = cache_k_l.shape[1]
    t = q.shape[1]
    k = jnp.concatenate([cache_k_l, k_new], axis=1)
    v = jnp.concatenate([cache_v_l, v_new], axis=1)
    return stick_breaking_block(q, past + jnp.arange(t), k, v, jnp.arange(past + t))


def spatial_gate(u, vn, w_s, b_s):
    B, T = vn.shape[0], vn.shape[1]
    L = min(T, SGU_LEN)
    nb = T // L
    c = jnp.arange(L) // CHUNK
    mask = c[None, :] <= c[:, None]
    w = jnp.where(mask, w_s[:, :L, :L], 0.0)
    vg = vn.reshape(B, nb, L, SGU_GROUPS, SGU_GROUP_DIM)
    s = jnp.einsum('gts,bnsgc->bntgc', w, vg) + b_s[:, :L].T[None, None, :, :, None]
    return u * s.reshape(B, T, SGU_WIDTH)


def layer(x, l, cache_k, cache_v, norm_mix, w_in, b_gate, sgu_norm, w_spatial, b_spatial,
          w_branch_a, w_branch_b, w_out, norm_ffn, w_gate_up, w_down):
    B, T = x.shape[0], x.shape[1]
    xn = rmsnorm(x, norm_mix[l])
    proj = xn @ w_in[l]
    q, k, v, u, vs, g = jnp.split(proj, [SB_WIDTH, 2 * SB_WIDTH, 3 * SB_WIDTH,
                                         3 * SB_WIDTH + SGU_WIDTH, 3 * SB_WIDTH + 2 * SGU_WIDTH], axis=-1)
    q = q.reshape(B, T, SB_HEADS, SB_HEAD_DIM)
    k = k.reshape(B, T, SB_HEADS, SB_HEAD_DIM)
    v = v.reshape(B, T, SB_HEADS, SB_HEAD_DIM)
    if cache_k is None:
        a = stick_breaking_prompt(q, k, v)
    else:
        a = stick_breaking_sample(q, k, v, cache_k[l], cache_v[l])
    a = a.reshape(B, T, SB_WIDTH)
    u = jax.nn.gelu(u)
    vs = rmsnorm(jax.nn.gelu(vs), sgu_norm[l])
    bb = spatial_gate(u, vs, w_spatial[l], b_spatial[l])
    gates = jax.nn.sigmoid(g + b_gate[l])
    g_a, g_b = gates[..., :D_MODEL], gates[..., D_MODEL:]
    merged = g_a * (a @ w_branch_a[l]) + g_b * (bb @ w_branch_b[l])
    h = x + merged @ w_out[l]
    hn = rmsnorm(h, norm_ffn[l])
    gu = hn @ w_gate_up[l]
    h = h + (jax.nn.silu(gu[..., :D_FF]) * gu[..., D_FF:]) @ w_down[l]
    return h, k, v, vs


def trunk(x, cache_k, cache_v, norm_mix, w_in, b_gate, sgu_norm, w_spatial, b_spatial,
          w_branch_a, w_branch_b, w_out, norm_ffn, w_gate_up, w_down, norm_final):
    ks, vs_list, svs = [], [], []
    for l in range(DEPTH):
        x, k, v, sv = layer(x, l, cache_k, cache_v, norm_mix, w_in, b_gate, sgu_norm, w_spatial,
                            b_spatial, w_branch_a, w_branch_b, w_out, norm_ffn, w_gate_up, w_down)
        ks.append(k)
        vs_list.append(v)
        svs.append(sv)
    return rmsnorm(x, norm_final), jnp.stack(ks), jnp.stack(vs_list), jnp.stack(svs)


def setup_inputs(seed: int = 0) -> dict:
    key = jax.random.key(seed)
    ks = jax.random.split(key, 20)
    f32 = jnp.float32
    n = lambda k, shape, s: (jax.random.normal(k, shape, f32) * s)
    return {
        "x_prompt": n(ks[0], (BATCH, SEQ, D_MODEL), 1.0),
        "x_sample": n(ks[1], (DEC_BATCH, DEC_SEQ, D_MODEL), 1.0),
        "cache_k": n(ks[2], (DEPTH, DEC_BATCH, PAST_LEN, SB_HEADS, SB_HEAD_DIM), 1.0),
        "cache_v": n(ks[3], (DEPTH, DEC_BATCH, PAST_LEN, SB_HEADS, SB_HEAD_DIM), 1.0),
        "norm_mix": 1.0 + n(ks[4], (DEPTH, D_MODEL), 0.02),
        "w_in": n(ks[5], (DEPTH, D_MODEL, IN_COLS), D_MODEL ** -0.5),
        "b_gate": n(ks[6], (DEPTH, 2 * D_MODEL), 0.02),
        "sgu_norm": 1.0 + n(ks[7], (DEPTH, SGU_WIDTH), 0.02),
        "w_spatial": n(ks[8], (DEPTH, SGU_GROUPS, SGU_LEN, SGU_LEN), SGU_LEN ** -0.5),
        "b_spatial": 1.0 + n(ks[9], (DEPTH, SGU_GROUPS, SGU_LEN), 0.1),
        "w_branch_a": n(ks[10], (DEPTH, SB_WIDTH, D_MODEL), SB_WIDTH ** -0.5),
        "w_branch_b": n(ks[11], (DEPTH, SGU_WIDTH, D_MODEL), SGU_WIDTH ** -0.5),
        "w_out": n(ks[12], (DEPTH, D_MODEL, D_MODEL), D_MODEL ** -0.5),
        "norm_ffn": 1.0 + n(ks[13], (DEPTH, D_MODEL), 0.02),
        "w_gate_up": n(ks[14], (DEPTH, D_MODEL, 2 * D_FF), D_MODEL ** -0.5),
        "w_down": n(ks[15], (DEPTH, D_FF, D_MODEL), D_FF ** -0.5),
        "norm_final": 1.0 + n(ks[16], (D_MODEL,), 0.02),
    }


def reference(x_prompt, x_sample, cache_k, cache_v, norm_mix, w_in, b_gate, sgu_norm, w_spatial,
              b_spatial, w_branch_a, w_branch_b, w_out, norm_ffn, w_gate_up, w_down, norm_final):
    y_prompt, k_prompt, v_prompt, _ = trunk(
        x_prompt, None, None, norm_mix, w_in, b_gate, sgu_norm, w_spatial, b_spatial,
        w_branch_a, w_branch_b, w_out, norm_ffn, w_gate_up, w_down, norm_final)
    y_sample, k_sample, v_sample, sgu_v_sample = trunk(
        x_sample, cache_k, cache_v, norm_mix, w_in, b_gate, sgu_norm, w_spatial, b_spatial,
        w_branch_a, w_branch_b, w_out, norm_ffn, w_gate_up, w_down, norm_final)
    return (y_prompt, y_sample, k_prompt, v_prompt, k_sample, v_sample, sgu_v_sample)
```

```python
import functools

import jax
import jax.numpy as jnp
from jax import lax
from jax.experimental import pallas as pl
from jax.experimental.pallas import tpu as pltpu

D_MODEL = 2048
HEAD_DIM = 128
SB_HEADS = 8
SB_WIDTH = SB_HEADS * HEAD_DIM
SGU_GROUPS = 8
SGU_GROUP_DIM = 128
SGU_WIDTH = SGU_GROUPS * SGU_GROUP_DIM
SGU_LEN = 128
CHUNK = 64
D_FF = 5632
EPS = 1e-6
KEY_BLOCK = 128
SEG = 1024
N_PROJ_TILES = (3 * SB_WIDTH + 2 * SGU_WIDTH + 2 * D_MODEL) // SEG
GATE_TILE0 = (3 * SB_WIDTH + 2 * SGU_WIDTH) // SEG
VMEM_LIMIT = 56 * 1024 * 1024

bf16 = jnp.bfloat16
f32 = jnp.float32


def _rmsnorm_rows(x, g):
    return x * lax.rsqrt(jnp.mean(x * x, axis=-1, keepdims=True) + EPS) * g


def _store_head_rows(dst_ref, acc):
    tm = acc.shape[0]
    for h in range(SB_HEADS):
        dst_ref[pl.ds(h, tm, stride=SB_HEADS), :] = acc[:, h * HEAD_DIM:(h + 1) * HEAD_DIM]


def _proj_kernel(x_ref, g_ref, w_ref, bg_ref, sn_ref, *rest):
    q_ref, kb_ref, vb_ref, k_ref, v_ref, u_ref, vn_ref, gate_ref, xn_ref = rest[-9:]
    j = pl.program_id(1)

    @pl.when(j == 0)
    def _():
        xn_ref[...] = _rmsnorm_rows(x_ref[...], g_ref[...]).astype(bf16)

    acc = jnp.dot(xn_ref[...], w_ref[...], preferred_element_type=f32)

    @pl.when(j == 0)
    def _():
        q_ref[...] = (acc * (HEAD_DIM ** -0.5)).astype(bf16)

    @pl.when(j == 1)
    def _():
        kb_ref[...] = acc.astype(bf16)
        _store_head_rows(k_ref, acc)

    @pl.when(j == 2)
    def _():
        vb_ref[...] = acc.astype(bf16)
        _store_head_rows(v_ref, acc)

    @pl.when(j == 3)
    def _():
        u_ref[...] = jax.nn.gelu(acc).astype(bf16)

    @pl.when(j == 4)
    def _():
        vn_ref[...] = _rmsnorm_rows(jax.nn.gelu(acc), sn_ref[...])

    @pl.when(j >= GATE_TILE0)
    def _():
        gate_ref[...] = jax.nn.sigmoid(acc + bg_ref[...]).astype(bf16)


def _proj(x, g, w, bg, sn, stacked, layer, depth, tm):
    m = x.shape[0]
    nm = m // tm
    gate_col = lambda i, j: (i, jnp.maximum(j - GATE_TILE0, 0))
    row = lambda i, j: (i, 0)
    layer_row = lambda i, j: (layer * nm + i, 0)
    in_specs = [
        pl.BlockSpec((tm, D_MODEL), row),
        pl.BlockSpec((1, D_MODEL), lambda i, j: (0, 0)),
        pl.BlockSpec((None, D_MODEL, SEG), lambda i, j: (layer, 0, j)),
        pl.BlockSpec((1, SEG), lambda i, j: (0, jnp.maximum(j - GATE_TILE0, 0))),
        pl.BlockSpec((1, SEG), lambda i, j: (0, 0)),
    ]
    args = [x, g, w, bg, sn]
    aliases = {}
    if stacked is not None:
        in_specs += [pl.BlockSpec(memory_space=pl.ANY)] * 3
        args += list(stacked)
        aliases = {5: 3, 6: 4, 7: 6}
    return pl.pallas_call(
        _proj_kernel,
        grid=(nm, N_PROJ_TILES),
        in_specs=in_specs,
        out_specs=[
            pl.BlockSpec((tm, SEG), row),
            pl.BlockSpec((tm, SEG), row),
            pl.BlockSpec((tm, SEG), row),
            pl.BlockSpec((tm * SB_HEADS, HEAD_DIM), layer_row),
            pl.BlockSpec((tm * SB_HEADS, HEAD_DIM), layer_row),
            pl.BlockSpec((tm, SEG), row),
            pl.BlockSpec((tm, SEG), layer_row),
            pl.BlockSpec((tm, SEG), gate_col),
        ],
        out_shape=[
            jax.ShapeDtypeStruct((m, SB_WIDTH), bf16),
            jax.ShapeDtypeStruct((m, SB_WIDTH), bf16),
            jax.ShapeDtypeStruct((m, SB_WIDTH), bf16),
            jax.ShapeDtypeStruct((depth * m * SB_HEADS, HEAD_DIM), f32),
            jax.ShapeDtypeStruct((depth * m * SB_HEADS, HEAD_DIM), f32),
            jax.ShapeDtypeStruct((m, SGU_WIDTH), bf16),
            jax.ShapeDtypeStruct((depth * m, SGU_WIDTH), f32),
            jax.ShapeDtypeStruct((m, 2 * D_MODEL), bf16),
        ],
        input_output_aliases=aliases,
        scratch_shapes=[pltpu.VMEM((tm, D_MODEL), bf16)],
        compiler_params=pltpu.CompilerParams(
            dimension_semantics=("arbitrary", "arbitrary"), vmem_limit_bytes=VMEM_LIMIT),
        name="proj",
    )(*args)


def _suffix_matrix():
    i = jnp.arange(KEY_BLOCK)[:, None]
    jj = jnp.arange(KEY_BLOCK)[None, :]
    strict = (i > jj).astype(bf16)
    half = jnp.concatenate([strict, jnp.ones((KEY_BLOCK, KEY_BLOCK), bf16)], axis=1)
    return jnp.concatenate([half, half], axis=0)


def _sb_block(q, kblk, vblk, c, tri, mask):
    z = lax.dot_general(q, kblk, (((1,), (1,)), ((), ())), preferred_element_type=f32)
    log_beta = jnp.minimum(z, 0.0) - jnp.log(1.0 + jnp.exp(-jnp.abs(z)))
    log_stay = log_beta - z
    if mask is not None:
        log_stay = jnp.where(mask, log_stay, 0.0)
    hi = log_stay.astype(bf16)
    lo = (log_stay - hi.astype(f32)).astype(bf16)
    sums = jnp.dot(jnp.concatenate([hi, lo], axis=1), tri, preferred_element_type=f32)
    w = jnp.exp(log_beta + sums[:, :KEY_BLOCK] + c)
    if mask is not None:
        w = jnp.where(mask, w, 0.0)
    contrib = jnp.dot(w.astype(bf16), vblk, preferred_element_type=f32)
    return contrib, c + sums[:, KEY_BLOCK:]


def _attn_prompt_kernel(q_ref, kb_ref, vb_ref, tri_ref, o_ref):
    tri = tri_ref[...]
    nblk = q_ref.shape[0] // KEY_BLOCK
    rows = lax.broadcasted_iota(jnp.int32, (KEY_BLOCK, KEY_BLOCK), 0)
    cols = lax.broadcasted_iota(jnp.int32, (KEY_BLOCK, KEY_BLOCK), 1)
    diag_mask = cols < rows

    def q_body(qb, carry):
        qs = pl.multiple_of(qb * KEY_BLOCK, KEY_BLOCK)
        q = q_ref[pl.ds(qs, KEY_BLOCK), :]
        acc0, c0 = _sb_block(q, kb_ref[pl.ds(qs, KEY_BLOCK), :], vb_ref[pl.ds(qs, KEY_BLOCK), :],
                             jnp.zeros((KEY_BLOCK, KEY_BLOCK), f32), tri, diag_mask)

        def k_body(t, kc):
            acc, c = kc
            ks = pl.multiple_of((qb - 1 - t) * KEY_BLOCK, KEY_BLOCK)
            contrib, c = _sb_block(q, kb_ref[pl.ds(ks, KEY_BLOCK), :], vb_ref[pl.ds(ks, KEY_BLOCK), :],
                                   c, tri, None)
            return acc + contrib, c

        acc, _ = lax.fori_loop(0, qb, k_body, (acc0, c0))
        o_ref[pl.ds(qs, KEY_BLOCK), :] = acc.astype(bf16)
        return carry

    lax.fori_loop(0, nblk, q_body, 0)


def _attn_prompt(q, k, v, tri, batch, seq):
    q3 = q.reshape(batch, seq, SB_WIDTH)
    k3 = k.reshape(batch, seq, SB_WIDTH)
    v3 = v.reshape(batch, seq, SB_WIDTH)
    head = pl.BlockSpec((None, seq, HEAD_DIM), lambda b, h: (b, 0, h))
    out = pl.pallas_call(
        _attn_prompt_kernel,
        grid=(batch, SB_HEADS),
        in_specs=[head, head, head,
                  pl.BlockSpec((2 * KEY_BLOCK, 2 * KEY_BLOCK), lambda b, h: (0, 0))],
        out_specs=head,
        out_shape=jax.ShapeDtypeStruct((batch, seq, SB_WIDTH), bf16),
        compiler_params=pltpu.CompilerParams(
            dimension_semantics=("arbitrary", "arbitrary"), vmem_limit_bytes=VMEM_LIMIT),
        name="attn_prompt",
    )(q3, k3, v3, tri)
    return out.reshape(batch * seq, SB_WIDTH)


def _attn_sample_kernel(q_ref, kn_ref, vn_ref, ck_ref, cv_ref, tri_ref, o_ref, kpad_ref, vpad_ref):
    h = pl.program_id(1)
    t_new = q_ref.shape[0]
    n_cache = ck_ref.shape[0] // (KEY_BLOCK * SB_HEADS)
    tri = tri_ref[...]
    kpad_ref[...] = jnp.zeros(kpad_ref.shape, bf16)
    vpad_ref[...] = jnp.zeros(vpad_ref.shape, bf16)
    kpad_ref[0:t_new, :] = kn_ref[...]
    vpad_ref[0:t_new, :] = vn_ref[...]
    rows = lax.broadcasted_iota(jnp.int32, (t_new, KEY_BLOCK), 0)
    cols = lax.broadcasted_iota(jnp.int32, (t_new, KEY_BLOCK), 1)
    q = q_ref[...]
    acc0, c0 = _sb_block(q, kpad_ref[...], vpad_ref[...], jnp.zeros((t_new, KEY_BLOCK), f32), tri, cols < rows)

    def k_body(t, kc):
        acc, c = kc
        head_rows = pl.ds((n_cache - 1 - t) * (KEY_BLOCK * SB_HEADS) + h, KEY_BLOCK, stride=SB_HEADS)
        contrib, c = _sb_block(q, ck_ref[head_rows, :].astype(bf16), cv_ref[head_rows, :].astype(bf16),
                               c, tri, None)
        return acc + contrib, c

    acc, _ = lax.fori_loop(0, n_cache, k_body, (acc0, c0))
    o_ref[...] = acc.astype(bf16)


def _attn_sample(q, k, v, cache_k, cache_v, layer, tri, batch, t_new):
    past = cache_k.shape[2]
    assert t_new <= KEY_BLOCK and past % KEY_BLOCK == 0
    q3 = q.reshape(batch, t_new, SB_WIDTH)
    k3 = k.reshape(batch, t_new, SB_WIDTH)
    v3 = v.reshape(batch, t_new, SB_WIDTH)
    ck = cache_k.reshape(cache_k.shape[0], batch, past * SB_HEADS, HEAD_DIM)
    cv = cache_v.reshape(cache_v.shape[0], batch, past * SB_HEADS, HEAD_DIM)
    new = pl.BlockSpec((None, t_new, HEAD_DIM), lambda b, h: (b, 0, h))
    old = pl.BlockSpec((None, None, past * SB_HEADS, HEAD_DIM), lambda b, h: (layer, b, 0, 0))
    out = pl.pallas_call(
        _attn_sample_kernel,
        grid=(batch, SB_HEADS),
        in_specs=[new, new, new, old, old,
                  pl.BlockSpec((2 * KEY_BLOCK, 2 * KEY_BLOCK), lambda b, h: (0, 0))],
        out_specs=new,
        out_shape=jax.ShapeDtypeStruct((batch, t_new, SB_WIDTH), bf16),
        scratch_shapes=[pltpu.VMEM((KEY_BLOCK, HEAD_DIM), bf16), pltpu.VMEM((KEY_BLOCK, HEAD_DIM), bf16)],
        compiler_params=pltpu.CompilerParams(
            dimension_semantics=("arbitrary", "arbitrary"), vmem_limit_bytes=VMEM_LIMIT),
        name="attn_sample",
    )(q3, k3, v3, ck, cv, tri)
    return out.reshape(batch * t_new, SB_WIDTH)


def _merge_kernel(a_ref, u_ref, vn_ref, gate_ref, x_ref, ws_ref, bs_ref, wa_ref, wb_ref, wo_ref,
                  h_ref, bb_ref, *, chunk_len):
    tm = a_ref.shape[0]
    rows = lax.broadcasted_iota(jnp.int32, (chunk_len, chunk_len), 0) // CHUNK
    cols = lax.broadcasted_iota(jnp.int32, (chunk_len, chunk_len), 1) // CHUNK
    causal = cols <= rows
    for g in range(SGU_GROUPS):
        w_g = jnp.where(causal, ws_ref[g], 0.0).astype(bf16)
        b_g = bs_ref[g]
        lanes = slice(g * SGU_GROUP_DIM, (g + 1) * SGU_GROUP_DIM)
        for n in range(tm // chunk_len):
            r = slice(n * chunk_len, (n + 1) * chunk_len)
            s = jnp.dot(w_g, vn_ref[r, lanes].astype(bf16), preferred_element_type=f32) + b_g
            bb_ref[r, lanes] = (u_ref[r, lanes].astype(f32) * s).astype(bf16)
    ta = jnp.dot(a_ref[...], wa_ref[...], preferred_element_type=f32)
    tb = jnp.dot(bb_ref[...], wb_ref[...], preferred_element_type=f32)
    merged = (gate_ref[:, :D_MODEL].astype(f32) * ta + gate_ref[:, D_MODEL:].astype(f32) * tb).astype(bf16)
    h_ref[...] = x_ref[...] + jnp.dot(merged, wo_ref[...], preferred_element_type=f32)


def _merge(a, u, vn_all, gates, x, ws, bs, wa, wb, wo, layer, chunk_len, tm):
    m = x.shape[0]
    row = lambda i: (i, 0)
    layer_row = lambda i: (layer * (m // tm) + i, 0)
    weight = lambda i: (layer, 0, 0)
    const3 = lambda i: (0, 0, 0)
    once = pl.Buffered(1)
    return pl.pallas_call(
        functools.partial(_merge_kernel, chunk_len=chunk_len),
        grid=(m // tm,),
        in_specs=[
            pl.BlockSpec((tm, SB_WIDTH), row),
            pl.BlockSpec((tm, SGU_WIDTH), row),
            pl.BlockSpec((tm, SGU_WIDTH), layer_row),
            pl.BlockSpec((tm, 2 * D_MODEL), row),
            pl.BlockSpec((tm, D_MODEL), row),
            pl.BlockSpec((SGU_GROUPS, chunk_len, chunk_len), const3, pipeline_mode=once),
            pl.BlockSpec((SGU_GROUPS, chunk_len, SGU_GROUP_DIM), const3, pipeline_mode=once),
            pl.BlockSpec((None, SB_WIDTH, D_MODEL), weight, pipeline_mode=once),
            pl.BlockSpec((None, SGU_WIDTH, D_MODEL), weight, pipeline_mode=once),
            pl.BlockSpec((None, D_MODEL, D_MODEL), weight, pipeline_mode=once),
        ],
        out_specs=pl.BlockSpec((tm, D_MODEL), row),
        out_shape=jax.ShapeDtypeStruct((m, D_MODEL), f32),
        scratch_shapes=[pltpu.VMEM((tm, SGU_WIDTH), bf16)],
        compiler_params=pltpu.CompilerParams(
            dimension_semantics=("arbitrary",), vmem_limit_bytes=VMEM_LIMIT),
        name="merge",
    )(a, u, vn_all, gates, x, ws, bs, wa, wb, wo)


def _ffn_kernel(h_ref, g_ref, wg_ref, wu_ref, wd_ref, gf_ref, o_ref, hn_ref, *, final_norm):
    j = pl.program_id(1)

    @pl.when(j == 0)
    def _():
        hn_ref[...] = _rmsnorm_rows(h_ref[...], g_ref[...]).astype(bf16)
        o_ref[...] = h_ref[...]

    hn = hn_ref[...]
    gate = jnp.dot(hn, wg_ref[...], preferred_element_type=f32)
    up = jnp.dot(hn, wu_ref[...], preferred_element_type=f32)
    act = (jax.nn.silu(gate) * up).astype(bf16)
    o_ref[...] += jnp.dot(act, wd_ref[...], preferred_element_type=f32)

    if final_norm:
        @pl.when(j == pl.num_programs(1) - 1)
        def _():
            o_ref[...] = _rmsnorm_rows(o_ref[...], gf_ref[...])


def _ffn(h, g, w_gate_up, w_down, g_final, layer, final_norm, tm, tf):
    m = h.shape[0]
    nf = D_FF // tf
    row = lambda i, j: (i, 0)
    vec = pl.BlockSpec((1, D_MODEL), lambda i, j: (0, 0))
    return pl.pallas_call(
        functools.partial(_ffn_kernel, final_norm=final_norm),
        grid=(m // tm, nf),
        in_specs=[
            pl.BlockSpec((tm, D_MODEL), row),
            vec,
            pl.BlockSpec((None, D_MODEL, tf), lambda i, j: (layer, 0, j)),
            pl.BlockSpec((None, D_MODEL, tf), lambda i, j: (layer, 0, j + nf)),
            pl.BlockSpec((None, tf, D_MODEL), lambda i, j: (layer, j, 0)),
            vec,
        ],
        out_specs=pl.BlockSpec((tm, D_MODEL), row),
        out_shape=jax.ShapeDtypeStruct((m, D_MODEL), f32),
        scratch_shapes=[pltpu.VMEM((tm, D_MODEL), bf16)],
        compiler_params=pltpu.CompilerParams(
            dimension_semantics=("arbitrary", "arbitrary"), vmem_limit_bytes=VMEM_LIMIT),
        name="ffn",
    )(h, g, w_gate_up, w_gate_up, w_down, g_final)


def _trunk(x, cache_k, cache_v, p, tri):
    batch, t = x.shape[0], x.shape[1]
    m = batch * t
    depth = p["w_in"].shape[0]
    chunk_len = min(t, SGU_LEN)
    h = x.reshape(m, D_MODEL)
    tm = min(512, m)
    stacked = None
    for l in range(depth):
        q, kb, vb, k_all, v_all, u, vn_all, gates = _proj(
            h, p["norm_mix"][l][None], p["w_in"], p["b_gate"][l][None], p["sgu_norm"][l][None],
            stacked, l, depth, tm=tm)
        stacked = (k_all, v_all, vn_all)
        if cache_k is None:
            a = _attn_prompt(q, kb, vb, tri, batch, t)
        else:
            a = _attn_sample(q, kb, vb, cache_k, cache_v, l, tri, batch, t)
        bs = jnp.broadcast_to(p["b_spatial"][l][:, :chunk_len, None], (SGU_GROUPS, chunk_len, SGU_GROUP_DIM))
        h = _merge(a, u, vn_all, gates, h, p["w_spatial"][l][:, :chunk_len, :chunk_len], bs,
                   p["w_branch_a"], p["w_branch_b"], p["w_out"], l, chunk_len, tm=min(256, m))
        h = _ffn(h, p["norm_ffn"][l][None], p["w_gate_up"], p["w_down"], p["norm_final"][None],
                 l, final_norm=(l == depth - 1), tm=tm, tf=512)
    k_all, v_all, vn_all = stacked
    return (h.reshape(batch, t, D_MODEL), k_all.reshape(depth, batch, t, SB_HEADS, HEAD_DIM),
            v_all.reshape(depth, batch, t, SB_HEADS, HEAD_DIM), vn_all.reshape(depth, batch, t, SGU_WIDTH))


def kernel(x_prompt, x_sample, cache_k, cache_v, norm_mix, w_in, b_gate, sgu_norm, w_spatial, b_spatial,
           w_branch_a, w_branch_b, w_out, norm_ffn, w_gate_up, w_down, norm_final):
    p = dict(
        norm_mix=norm_mix, b_gate=b_gate, sgu_norm=sgu_norm, w_spatial=w_spatial, b_spatial=b_spatial,
        norm_ffn=norm_ffn, norm_final=norm_final,
        w_in=w_in.astype(bf16), w_branch_a=w_branch_a.astype(bf16), w_branch_b=w_branch_b.astype(bf16),
        w_out=w_out.astype(bf16), w_gate_up=w_gate_up.astype(bf16), w_down=w_down.astype(bf16),
    )
    tri = _suffix_matrix()
    y_prompt, k_prompt, v_prompt, _ = _trunk(x_prompt, None, None, p, tri)
    y_sample, k_sample, v_sample, sgu_v_sample = _trunk(x_sample, cache_k, cache_v, p, tri)
    return (y_prompt, y_sample, k_prompt, v_prompt, k_sample, v_sample, sgu_v_sample)
```

```python
import functools

import jax
import jax.numpy as jnp
from jax import lax
from jax.experimental import pallas as pl
from jax.experimental.pallas import tpu as pltpu

D_MODEL = 2048
HEAD_DIM = 128
SB_HEADS = 8
SB_WIDTH = SB_HEADS * HEAD_DIM
SGU_GROUPS = 8
SGU_GROUP_DIM = 128
SGU_WIDTH = SGU_GROUPS * SGU_GROUP_DIM
SGU_LEN = 128
CHUNK = 64
D_FF = 5632
EPS = 1e-6
KEY_BLOCK = 128
CACHE_BLOCK_ROWS = KEY_BLOCK * SB_HEADS
TAIL_BLOCKS = 2
SAMPLE_STREAMS = 2
WEIGHT_CUTOFF = -105.0
SEG = 1024
N_PROJ_TILES = (3 * SB_WIDTH + 2 * SGU_WIDTH + 2 * D_MODEL) // SEG
GATE_TILE0 = (3 * SB_WIDTH + 2 * SGU_WIDTH) // SEG
VMEM_LIMIT = 56 * 1024 * 1024

bf16 = jnp.bfloat16
f32 = jnp.float32


def _rmsnorm_rows(x, g):
    return x * lax.rsqrt(jnp.mean(x * x, axis=-1, keepdims=True) + EPS) * g


def _store_head_rows(dst_ref, acc):
    tm = acc.shape[0]
    for h in range(SB_HEADS):
        dst_ref[pl.ds(h, tm, stride=SB_HEADS), :] = acc[:, h * HEAD_DIM:(h + 1) * HEAD_DIM]


def _proj_kernel(x_ref, g_ref, w_ref, bg_ref, sn_ref, *rest):
    q_ref, kb_ref, vb_ref, k_ref, v_ref, u_ref, vn_ref, gate_ref, xn_ref = rest[-9:]
    j = pl.program_id(1)

    @pl.when(j == 0)
    def _():
        xn_ref[...] = _rmsnorm_rows(x_ref[...], g_ref[...]).astype(bf16)

    acc = jnp.dot(xn_ref[...], w_ref[...], preferred_element_type=f32)

    @pl.when(j == 0)
    def _():
        q_ref[...] = (acc * (HEAD_DIM ** -0.5)).astype(bf16)

    @pl.when(j == 1)
    def _():
        kb_ref[...] = acc.astype(bf16)
        _store_head_rows(k_ref, acc)

    @pl.when(j == 2)
    def _():
        vb_ref[...] = acc.astype(bf16)
        _store_head_rows(v_ref, acc)

    @pl.when(j == 3)
    def _():
        u_ref[...] = jax.nn.gelu(acc).astype(bf16)

    @pl.when(j == 4)
    def _():
        vn_ref[...] = _rmsnorm_rows(jax.nn.gelu(acc), sn_ref[...])

    @pl.when(j >= GATE_TILE0)
    def _():
        gate_ref[...] = jax.nn.sigmoid(acc + bg_ref[...]).astype(bf16)


def _proj(x, g, w, bg, sn, stacked, layer, depth, tm):
    m = x.shape[0]
    nm = m // tm
    gate_col = lambda i, j: (i, jnp.maximum(j - GATE_TILE0, 0))
    row = lambda i, j: (i, 0)
    layer_row = lambda i, j: (layer * nm + i, 0)
    in_specs = [
        pl.BlockSpec((tm, D_MODEL), row),
        pl.BlockSpec((1, D_MODEL), lambda i, j: (0, 0)),
        pl.BlockSpec((None, D_MODEL, SEG), lambda i, j: (layer, 0, j)),
        pl.BlockSpec((1, SEG), lambda i, j: (0, jnp.maximum(j - GATE_TILE0, 0))),
        pl.BlockSpec((1, SEG), lambda i, j: (0, 0)),
    ]
    args = [x, g, w, bg, sn]
    aliases = {}
    if stacked is not None:
        in_specs += [pl.BlockSpec(memory_space=pl.ANY)] * 3
        args += list(stacked)
        aliases = {5: 3, 6: 4, 7: 6}
    return pl.pallas_call(
        _proj_kernel,
        grid=(nm, N_PROJ_TILES),
        in_specs=in_specs,
        out_specs=[
            pl.BlockSpec((tm, SEG), row),
            pl.BlockSpec((tm, SEG), row),
            pl.BlockSpec((tm, SEG), row),
            pl.BlockSpec((tm * SB_HEADS, HEAD_DIM), layer_row),
            pl.BlockSpec((tm * SB_HEADS, HEAD_DIM), layer_row),
            pl.BlockSpec((tm, SEG), row),
            pl.BlockSpec((tm, SEG), layer_row),
            pl.BlockSpec((tm, SEG), gate_col),
        ],
        out_shape=[
            jax.ShapeDtypeStruct((m, SB_WIDTH), bf16),
            jax.ShapeDtypeStruct((m, SB_WIDTH), bf16),
            jax.ShapeDtypeStruct((m, SB_WIDTH), bf16),
            jax.ShapeDtypeStruct((depth * m * SB_HEADS, HEAD_DIM), f32),
            jax.ShapeDtypeStruct((depth * m * SB_HEADS, HEAD_DIM), f32),
            jax.ShapeDtypeStruct((m, SGU_WIDTH), bf16),
            jax.ShapeDtypeStruct((depth * m, SGU_WIDTH), f32),
            jax.ShapeDtypeStruct((m, 2 * D_MODEL), bf16),
        ],
        input_output_aliases=aliases,
        scratch_shapes=[pltpu.VMEM((tm, D_MODEL), bf16)],
        compiler_params=pltpu.CompilerParams(
            dimension_semantics=("arbitrary", "arbitrary"), vmem_limit_bytes=VMEM_LIMIT),
        name="proj",
    )(*args)


def _suffix_matrix():
    i = jnp.arange(KEY_BLOCK)[:, None]
    jj = jnp.arange(KEY_BLOCK)[None, :]
    strict = (i > jj).astype(bf16)
    half = jnp.concatenate([strict, jnp.ones((KEY_BLOCK, KEY_BLOCK), bf16)], axis=1)
    return jnp.concatenate([half, half], axis=0)


def _sweep_step(chains, acc_ref, c_ref, tri, mask, first):
    zs = [lax.dot_general(q, kblk, (((1,), (1,)), ((), ())), preferred_element_type=f32)
          for q, kblk, _ in chains]
    log_betas, sums = [], []
    for z in zs:
        log_beta = jnp.minimum(z, 0.0) - jnp.log(1.0 + jnp.exp(-jnp.abs(z)))
        log_stay = log_beta - z
        if mask is not None:
            log_stay = jnp.where(mask, log_stay, 0.0)
        hi = log_stay.astype(bf16)
        lo = (log_stay - hi.astype(f32)).astype(bf16)
        log_betas.append(log_beta)
        sums.append(jnp.dot(jnp.concatenate([hi, lo], axis=1), tri, preferred_element_type=f32))
    worst = None
    contribs = []
    for i, (log_beta, s, (_, _, vblk)) in enumerate(zip(log_betas, sums, chains)):
        log_w = log_beta + s[:, :KEY_BLOCK]
        c = s[:, KEY_BLOCK:]
        if not first:
            carried = c_ref[i][...]
            log_w = log_w + carried
            c = c + carried
        c_ref[i][...] = c
        worst = c if worst is None else jnp.maximum(worst, c)
        w = jnp.exp(log_w)
        if mask is not None:
            w = jnp.where(mask, w, 0.0)
        contribs.append(jnp.dot(w.astype(bf16), vblk, preferred_element_type=f32))
    for i, contrib in enumerate(contribs):
        if first:
            acc_ref[i][...] = contrib
        else:
            acc_ref[i][...] += contrib
    return jnp.max(worst)


def _head_lanes(h):
    return slice(h * HEAD_DIM, (h + 1) * HEAD_DIM)


def _attn_prompt_kernel(q_ref, kb_ref, vb_ref, tri_ref, o_ref, *state):
    acc_ref, c_ref = state[:SB_HEADS], state[SB_HEADS:]
    qb = pl.program_id(1)
    tri = tri_ref[...]
    rows = lax.broadcasted_iota(jnp.int32, (KEY_BLOCK, KEY_BLOCK), 0)
    cols = lax.broadcasted_iota(jnp.int32, (KEY_BLOCK, KEY_BLOCK), 1)

    def chains(kblock):
        ks = pl.multiple_of(kblock * KEY_BLOCK, KEY_BLOCK)
        return [(q_ref[:, _head_lanes(h)], kb_ref[pl.ds(ks, KEY_BLOCK), _head_lanes(h)],
                 vb_ref[pl.ds(ks, KEY_BLOCK), _head_lanes(h)]) for h in range(SB_HEADS)]

    worst = _sweep_step(chains(qb), acc_ref, c_ref, tri, cols < rows, True)

    def more(state):
        kblock, worst = state
        return jnp.logical_and(kblock >= 0, worst > WEIGHT_CUTOFF)

    def step(state):
        kblock, _ = state
        return kblock - 1, _sweep_step(chains(kblock), acc_ref, c_ref, tri, None, False)

    lax.while_loop(more, step, (qb - 1, worst))
    for h in range(SB_HEADS):
        o_ref[:, _head_lanes(h)] = acc_ref[h][...].astype(bf16)


def _attn_prompt(q, k, v, tri, batch, seq):
    q3 = q.reshape(batch, seq, SB_WIDTH)
    k3 = k.reshape(batch, seq, SB_WIDTH)
    v3 = v.reshape(batch, seq, SB_WIDTH)
    qblock = pl.BlockSpec((None, KEY_BLOCK, SB_WIDTH), lambda b, i: (b, i, 0))
    whole = pl.BlockSpec((None, seq, SB_WIDTH), lambda b, i: (b, 0, 0))
    out = pl.pallas_call(
        _attn_prompt_kernel,
        grid=(batch, seq // KEY_BLOCK),
        in_specs=[qblock, whole, whole,
                  pl.BlockSpec((2 * KEY_BLOCK, 2 * KEY_BLOCK), lambda b, i: (0, 0))],
        out_specs=qblock,
        out_shape=jax.ShapeDtypeStruct((batch, seq, SB_WIDTH), bf16),
        scratch_shapes=[pltpu.VMEM((KEY_BLOCK, HEAD_DIM), f32)] * (2 * SB_HEADS),
        compiler_params=pltpu.CompilerParams(
            dimension_semantics=("arbitrary", "arbitrary"), vmem_limit_bytes=VMEM_LIMIT),
        name="attn_prompt",
    )(q3, k3, v3, tri)
    return out.reshape(batch * seq, SB_WIDTH)


def _attn_sample_kernel(q_ref, kn_ref, vn_ref, tk_ref, tv_ref, ck_hbm, cv_hbm, tri_ref, o_ref,
                        kpad_ref, vpad_ref, kold_ref, vold_ref, sem, *state, layer, n_cache):
    g = pl.program_id(0)
    streams, t_new = q_ref.shape[0], q_ref.shape[1]
    acc_ref, c_ref = state[:streams * SB_HEADS], state[streams * SB_HEADS:]
    n_tail = tk_ref.shape[1] // CACHE_BLOCK_ROWS
    tri = tri_ref[...]
    kpad_ref[...] = jnp.zeros(kpad_ref.shape, bf16)
    vpad_ref[...] = jnp.zeros(vpad_ref.shape, bf16)
    kpad_ref[:, 0:t_new, :] = kn_ref[...]
    vpad_ref[:, 0:t_new, :] = vn_ref[...]
    rows = lax.broadcasted_iota(jnp.int32, (t_new, KEY_BLOCK), 0)
    cols = lax.broadcasted_iota(jnp.int32, (t_new, KEY_BLOCK), 1)

    heads = [(s, h) for s in range(streams) for h in range(SB_HEADS)]
    new_chains = [(q_ref[s, :, _head_lanes(h)], kpad_ref[s, :, _head_lanes(h)], vpad_ref[s, :, _head_lanes(h)])
                  for s, h in heads]
    worst = _sweep_step(new_chains, acc_ref, c_ref, tri, cols < rows, True)

    def cache_chains(k_ref, v_ref, first_row):
        out = []
        for s, h in heads:
            head_rows = pl.ds(first_row + h, KEY_BLOCK, stride=SB_HEADS)
            out.append((q_ref[s, :, _head_lanes(h)], k_ref[s, head_rows, :].astype(bf16),
                        v_ref[s, head_rows, :].astype(bf16)))
        return out

    def more(state):
        left, worst = state
        return jnp.logical_and(left > 0, worst > WEIGHT_CUTOFF)

    def tail_step(state):
        left, _ = state
        first_row = pl.multiple_of((left - 1) * CACHE_BLOCK_ROWS, CACHE_BLOCK_ROWS)
        return left - 1, _sweep_step(cache_chains(tk_ref, tv_ref, first_row), acc_ref, c_ref, tri, None, False)

    _, worst = lax.while_loop(more, tail_step, (n_tail, worst))

    def old_copies(block):
        window = (layer, pl.ds(g * streams, streams), pl.ds(block * CACHE_BLOCK_ROWS, CACHE_BLOCK_ROWS))
        return (pltpu.make_async_copy(ck_hbm.at[window], kold_ref, sem.at[0]),
                pltpu.make_async_copy(cv_hbm.at[window], vold_ref, sem.at[1]))

    def old_step(state):
        left, _ = state
        copies = old_copies(left - 1)
        for cp in copies:
            cp.start()
        for cp in copies:
            cp.wait()
        return left - 1, _sweep_step(cache_chains(kold_ref, vold_ref, 0), acc_ref, c_ref, tri, None, False)

    lax.while_loop(more, old_step, (n_cache - n_tail, worst))
    for i, (s, h) in enumerate(heads):
        o_ref[s, :, _head_lanes(h)] = acc_ref[i][...].astype(bf16)


def _attn_sample(q, k, v, cache_k, cache_v, layer, tri, batch, t_new):
    past = cache_k.shape[2]
    assert t_new <= KEY_BLOCK and past % (KEY_BLOCK * TAIL_BLOCKS) == 0 and batch % SAMPLE_STREAMS == 0
    n_cache = past // KEY_BLOCK
    q3 = q.reshape(batch, t_new, SB_WIDTH)
    k3 = k.reshape(batch, t_new, SB_WIDTH)
    v3 = v.reshape(batch, t_new, SB_WIDTH)
    ck = cache_k.reshape(cache_k.shape[0], batch, past * SB_HEADS, HEAD_DIM)
    cv = cache_v.reshape(cache_v.shape[0], batch, past * SB_HEADS, HEAD_DIM)
    new = pl.BlockSpec((SAMPLE_STREAMS, t_new, SB_WIDTH), lambda g: (g, 0, 0))
    tail = pl.BlockSpec((None, SAMPLE_STREAMS, TAIL_BLOCKS * CACHE_BLOCK_ROWS, HEAD_DIM),
                        lambda g: (layer, g, n_cache // TAIL_BLOCKS - 1, 0))
    hbm = pl.BlockSpec(memory_space=pl.ANY)
    out = pl.pallas_call(
        functools.partial(_attn_sample_kernel, layer=layer, n_cache=n_cache),
        grid=(batch // SAMPLE_STREAMS,),
        in_specs=[new, new, new, tail, tail, hbm, hbm,
                  pl.BlockSpec((2 * KEY_BLOCK, 2 * KEY_BLOCK), lambda g: (0, 0))],
        out_specs=new,
        out_shape=jax.ShapeDtypeStruct((batch, t_new, SB_WIDTH), bf16),
        scratch_shapes=[
            pltpu.VMEM((SAMPLE_STREAMS, KEY_BLOCK, SB_WIDTH), bf16),
            pltpu.VMEM((SAMPLE_STREAMS, KEY_BLOCK, SB_WIDTH), bf16),
            pltpu.VMEM((SAMPLE_STREAMS, CACHE_BLOCK_ROWS, HEAD_DIM), f32),
            pltpu.VMEM((SAMPLE_STREAMS, CACHE_BLOCK_ROWS, HEAD_DIM), f32),
            pltpu.SemaphoreType.DMA((2,)),
        ] + [pltpu.VMEM((t_new, HEAD_DIM), f32)] * (2 * SAMPLE_STREAMS * SB_HEADS),
        compiler_params=pltpu.CompilerParams(
            dimension_semantics=("arbitrary",), vmem_limit_bytes=VMEM_LIMIT),
        name="attn_sample",
    )(q3, k3, v3, ck, cv, ck, cv, tri)
    return out.reshape(batch * t_new, SB_WIDTH)


def _merge_kernel(a_ref, u_ref, vn_ref, gate_ref, x_ref, ws_ref, bs_ref, wa_ref, wb_ref, wo_ref,
                  h_ref, bb_ref, *, chunk_len):
    tm = a_ref.shape[0]
    rows = lax.broadcasted_iota(jnp.int32, (chunk_len, chunk_len), 0) // CHUNK
    cols = lax.broadcasted_iota(jnp.int32, (chunk_len, chunk_len), 1) // CHUNK
    causal = cols <= rows
    for g in range(SGU_GROUPS):
        w_g = jnp.where(causal, ws_ref[g], 0.0).astype(bf16)
        b_g = bs_ref[g]
        lanes = slice(g * SGU_GROUP_DIM, (g + 1) * SGU_GROUP_DIM)
        for n in range(tm // chunk_len):
            r = slice(n * chunk_len, (n + 1) * chunk_len)
            s = jnp.dot(w_g, vn_ref[r, lanes].astype(bf16), preferred_element_type=f32) + b_g
            bb_ref[r, lanes] = (u_ref[r, lanes].astype(f32) * s).astype(bf16)
    ta = jnp.dot(a_ref[...], wa_ref[...], preferred_element_type=f32)
    tb = jnp.dot(bb_ref[...], wb_ref[...], preferred_element_type=f32)
    merged = (gate_ref[:, :D_MODEL].astype(f32) * ta + gate_ref[:, D_MODEL:].astype(f32) * tb).astype(bf16)
    h_ref[...] = x_ref[...] + jnp.dot(merged, wo_ref[...], preferred_element_type=f32)


def _merge(a, u, vn_all, gates, x, ws, bs, wa, wb, wo, layer, chunk_len, tm):
    m = x.shape[0]
    row = lambda i: (i, 0)
    layer_row = lambda i: (layer * (m // tm) + i, 0)
    weight = lambda i: (layer, 0, 0)
    const3 = lambda i: (0, 0, 0)
    once = pl.Buffered(1)
    return pl.pallas_call(
        functools.partial(_merge_kernel, chunk_len=chunk_len),
        grid=(m // tm,),
        in_specs=[
            pl.BlockSpec((tm, SB_WIDTH), row),
            pl.BlockSpec((tm, SGU_WIDTH), row),
            pl.BlockSpec((tm, SGU_WIDTH), layer_row),
            pl.BlockSpec((tm, 2 * D_MODEL), row),
            pl.BlockSpec((tm, D_MODEL), row),
            pl.BlockSpec((SGU_GROUPS, chunk_len, chunk_len), const3, pipeline_mode=once),
            pl.BlockSpec((SGU_GROUPS, chunk_len, SGU_GROUP_DIM), const3, pipeline_mode=once),
            pl.BlockSpec((None, SB_WIDTH, D_MODEL), weight, pipeline_mode=once),
            pl.BlockSpec((None, SGU_WIDTH, D_MODEL), weight, pipeline_mode=once),
            pl.BlockSpec((None, D_MODEL, D_MODEL), weight, pipeline_mode=once),
        ],
        out_specs=pl.BlockSpec((tm, D_MODEL), row),
        out_shape=jax.ShapeDtypeStruct((m, D_MODEL), f32),
        scratch_shapes=[pltpu.VMEM((tm, SGU_WIDTH), bf16)],
        compiler_params=pltpu.CompilerParams(
            dimension_semantics=("arbitrary",), vmem_limit_bytes=VMEM_LIMIT),
        name="merge",
    )(a, u, vn_all, gates, x, ws, bs, wa, wb, wo)


def _ffn_kernel(h_ref, g_ref, wg_ref, wu_ref, wd_ref, gf_ref, o_ref, hn_ref, *, final_norm):
    j = pl.program_id(1)

    @pl.when(j == 0)
    def _():
        hn_ref[...] = _rmsnorm_rows(h_ref[...], g_ref[...]).astype(bf16)
        o_ref[...] = h_ref[...]

    hn = hn_ref[...]
    gate = jnp.dot(hn, wg_ref[...], preferred_element_type=f32)
    up = jnp.dot(hn, wu_ref[...], preferred_element_type=f32)
    act = (jax.nn.silu(gate) * up).astype(bf16)
    o_ref[...] += jnp.dot(act, wd_ref[...], preferred_element_type=f32)

    if final_norm:
        @pl.when(j == pl.num_programs(1) - 1)
        def _():
            o_ref[...] = _rmsnorm_rows(o_ref[...], gf_ref[...])


def _ffn(h, g, w_gate_up, w_down, g_final, layer, final_norm, tm, tf):
    m = h.shape[0]
    nf = D_FF // tf
    row = lambda i, j: (i, 0)
    vec = pl.BlockSpec((1, D_MODEL), lambda i, j: (0, 0))
    return pl.pallas_call(
        functools.partial(_ffn_kernel, final_norm=final_norm),
        grid=(m // tm, nf),
        in_specs=[
            pl.BlockSpec((tm, D_MODEL), row),
            vec,
            pl.BlockSpec((None, D_MODEL, tf), lambda i, j: (layer, 0, j)),
            pl.BlockSpec((None, D_MODEL, tf), lambda i, j: (layer, 0, j + nf)),
            pl.BlockSpec((None, tf, D_MODEL), lambda i, j: (layer, j, 0)),
            vec,
        ],
        out_specs=pl.BlockSpec((tm, D_MODEL), row),
        out_shape=jax.ShapeDtypeStruct((m, D_MODEL), f32),
        scratch_shapes=[pltpu.VMEM((tm, D_MODEL), bf16)],
        compiler_params=pltpu.CompilerParams(
            dimension_semantics=("arbitrary", "arbitrary"), vmem_limit_bytes=VMEM_LIMIT),
        name="ffn",
    )(h, g, w_gate_up, w_gate_up, w_down, g_final)


def _trunk(x, cache_k, cache_v, p, tri):
    batch, t = x.shape[0], x.shape[1]
    m = batch * t
    depth = p["w_in"].shape[0]
    chunk_len = min(t, SGU_LEN)
    h = x.reshape(m, D_MODEL)
    tm = min(512, m)
    stacked = None
    for l in range(depth):
        q, kb, vb, k_all, v_all, u, vn_all, gates = _proj(
            h, p["norm_mix"][l][None], p["w_in"], p["b_gate"][l][None], p["sgu_norm"][l][None],
            stacked, l, depth, tm=tm)
        stacked = (k_all, v_all, vn_all)
        if cache_k is None:
            a = _attn_prompt(q, kb, vb, tri, batch, t)
        else:
            a = _attn_sample(q, kb, vb, cache_k, cache_v, l, tri, batch, t)
        bs = jnp.broadcast_to(p["b_spatial"][l][:, :chunk_len, None], (SGU_GROUPS, chunk_len, SGU_GROUP_DIM))
        h = _merge(a, u, vn_all, gates, h, p["w_spatial"][l][:, :chunk_len, :chunk_len], bs,
                   p["w_branch_a"], p["w_branch_b"], p["w_out"], l, chunk_len, tm=min(256, m))
        h = _ffn(h, p["norm_ffn"][l][None], p["w_gate_up"], p["w_down"], p["norm_final"][None],
                 l, final_norm=(l == depth - 1), tm=tm, tf=512)
    k_all, v_all, vn_all = stacked
    return (h.reshape(batch, t, D_MODEL), k_all.reshape(depth, batch, t, SB_HEADS, HEAD_DIM),
            v_all.reshape(depth, batch, t, SB_HEADS, HEAD_DIM), vn_all.reshape(depth, batch, t, SGU_WIDTH))


def kernel(x_prompt, x_sample, cache_k, cache_v, norm_mix, w_in, b_gate, sgu_norm, w_spatial, b_spatial,
           w_branch_a, w_branch_b, w_out, norm_ffn, w_gate_up, w_down, norm_final):
    p = dict(
        norm_mix=norm_mix, b_gate=b_gate, sgu_norm=sgu_norm, w_spatial=w_spatial, b_spatial=b_spatial,
        norm_ffn=norm_ffn, norm_final=norm_final,
        w_in=w_in.astype(bf16), w_branch_a=w_branch_a.astype(bf16), w_branch_b=w_branch_b.astype(bf16),
        w_out=w_out.astype(bf16), w_gate_up=w_gate_up.astype(bf16), w_down=w_down.astype(bf16),
    )
    tri = _suffix_matrix()
    y_prompt, k_prompt, v_prompt, _ = _trunk(x_prompt, None, None, p, tri)
    y_sample, k_sample, v_sample, sgu_v_sample = _trunk(x_sample, cache_k, cache_v, p, tri)
    return (y_prompt, y_sample, k_prompt, v_prompt, k_sample, v_sample, sgu_v_sample)
```

```python
import functools

import jax
import jax.numpy as jnp
from jax import lax
from jax.experimental import pallas as pl
from jax.experimental.pallas import tpu as pltpu

D_MODEL = 2048
HEAD_DIM = 128
SB_HEADS = 8
SB_WIDTH = SB_HEADS * HEAD_DIM
SGU_GROUPS = 8
SGU_GROUP_DIM = 128
SGU_WIDTH = SGU_GROUPS * SGU_GROUP_DIM
SGU_LEN = 128
CHUNK = 64
D_FF = 5632
EPS = 1e-6
KEY_BLOCK = 128
CACHE_BLOCK_ROWS = KEY_BLOCK * SB_HEADS
TAIL_BLOCKS = 2
SAMPLE_STREAMS = 2
WEIGHT_CUTOFF = -105.0
SEG = 1024
ROW_CHUNK = 256
BF16_SUBLANES = 16
N_PROJ_TILES = (3 * SB_WIDTH + 2 * SGU_WIDTH + 2 * D_MODEL) // SEG
GATE_TILE0 = (3 * SB_WIDTH + 2 * SGU_WIDTH) // SEG
VMEM_LIMIT = 56 * 1024 * 1024

bf16 = jnp.bfloat16
f32 = jnp.float32


def _rmsnorm_rows(x, g):
    return x * lax.rsqrt(jnp.mean(x * x, axis=-1, keepdims=True) + EPS) * g


def _store_head_rows(dst_ref, row0, acc):
    n = acc.shape[0]
    for h in range(SB_HEADS):
        dst_ref[pl.ds(row0 * SB_HEADS + h, n, stride=SB_HEADS), :] = acc[:, h * HEAD_DIM:(h + 1) * HEAD_DIM]


def _slab_rows(rows, n_steps):
    for slab in range(BF16_SUBLANES, rows + 1, BF16_SUBLANES):
        if rows % slab == 0 and rows // slab <= n_steps:
            return slab
    raise ValueError(f"no slab size covers {rows} rows in {n_steps} steps")


def _cast_rider(weights, layer, n_steps, step_of):
    in_specs, out_specs, out_shapes = [], [], []
    for w in weights:
        _, rows, cols = w.shape
        slab = _slab_rows(rows, n_steps)
        last = rows // slab - 1
        in_specs.append(pl.BlockSpec(
            (None, slab, cols), lambda *ids, last=last: (layer, jnp.minimum(step_of(*ids), last), 0)))
        out_specs.append(pl.BlockSpec(
            (slab, cols), lambda *ids, last=last: (jnp.minimum(step_of(*ids), last), 0)))
        out_shapes.append(jax.ShapeDtypeStruct((rows, cols), bf16))
    return in_specs, out_specs, out_shapes


def _run_cast_rider(src_refs, dst_refs):
    for src, dst in zip(src_refs, dst_refs):
        dst[...] = src[...].astype(bf16)


def _proj_kernel(*refs, n_alias, n_rider):
    x_ref, g_ref, w_ref, bg_ref, sn_ref = refs[:5]
    rider_src = refs[5 + n_alias:5 + n_alias + n_rider]
    outs = refs[5 + n_alias + n_rider:]
    q_ref, kb_ref, vb_ref, k_ref, v_ref, u_ref, vn_ref, gate_ref = outs[:8]
    rider_dst, xn_ref = outs[8:8 + n_rider], outs[8 + n_rider]
    j = pl.program_id(1)
    tm = x_ref.shape[0]

    def segment(emit, normalise=False):
        _run_cast_rider(rider_src, rider_dst)
        for r in range(0, tm, ROW_CHUNK):
            rows = slice(r, r + ROW_CHUNK)
            if normalise:
                xn_ref[rows, :] = _rmsnorm_rows(x_ref[rows, :], g_ref[...]).astype(bf16)
            emit(r, rows, jnp.dot(xn_ref[rows, :], w_ref[...], preferred_element_type=f32))

    @pl.when(j == 0)
    def _():
        def emit(r, rows, acc):
            q_ref[rows, :] = (acc * (HEAD_DIM ** -0.5)).astype(bf16)
        segment(emit, normalise=True)

    @pl.when(j == 1)
    def _():
        def emit(r, rows, acc):
            kb_ref[rows, :] = acc.astype(bf16)
            _store_head_rows(k_ref, r, acc)
        segment(emit)

    @pl.when(j == 2)
    def _():
        def emit(r, rows, acc):
            vb_ref[rows, :] = acc.astype(bf16)
            _store_head_rows(v_ref, r, acc)
        segment(emit)

    @pl.when(j == 3)
    def _():
        def emit(r, rows, acc):
            u_ref[rows, :] = jax.nn.gelu(acc).astype(bf16)
        segment(emit)

    @pl.when(j == 4)
    def _():
        def emit(r, rows, acc):
            vn_ref[rows, :] = _rmsnorm_rows(jax.nn.gelu(acc), sn_ref[...])
        segment(emit)

    @pl.when(j >= GATE_TILE0)
    def _():
        def emit(r, rows, acc):
            gate_ref[rows, :] = jax.nn.sigmoid(acc + bg_ref[...]).astype(bf16)
        segment(emit)


def _proj(x, g, w, bg, sn, stacked, to_cast, layer, depth, tm):
    m = x.shape[0]
    nm = m // tm
    gate_col = lambda i, j: (i, jnp.maximum(j - GATE_TILE0, 0))
    row = lambda i, j: (i, 0)
    layer_row = lambda i, j: (layer * nm + i, 0)
    in_specs = [
        pl.BlockSpec((tm, D_MODEL), row),
        pl.BlockSpec((1, D_MODEL), lambda i, j: (0, 0)),
        pl.BlockSpec((D_MODEL, SEG), lambda i, j: (0, j)),
        pl.BlockSpec((1, SEG), lambda i, j: (0, jnp.maximum(j - GATE_TILE0, 0))),
        pl.BlockSpec((1, SEG), lambda i, j: (0, 0)),
    ]
    args = [x, g, w, bg, sn]
    aliases = {}
    if stacked is not None:
        in_specs += [pl.BlockSpec(memory_space=pl.ANY)] * 3
        args += list(stacked)
        aliases = {5: 3, 6: 4, 7: 6}
    rider_in, rider_out, rider_shapes = _cast_rider(
        to_cast, layer, nm * N_PROJ_TILES, lambda i, j: i * N_PROJ_TILES + j)
    return pl.pallas_call(
        functools.partial(_proj_kernel, n_alias=len(args) - 5, n_rider=len(to_cast)),
        grid=(nm, N_PROJ_TILES),
        in_specs=in_specs + rider_in,
        out_specs=[
            pl.BlockSpec((tm, SEG), row),
            pl.BlockSpec((tm, SEG), row),
            pl.BlockSpec((tm, SEG), row),
            pl.BlockSpec((tm * SB_HEADS, HEAD_DIM), layer_row),
            pl.BlockSpec((tm * SB_HEADS, HEAD_DIM), layer_row),
            pl.BlockSpec((tm, SEG), row),
            pl.BlockSpec((tm, SEG), layer_row),
            pl.BlockSpec((tm, SEG), gate_col),
        ] + rider_out,
        out_shape=[
            jax.ShapeDtypeStruct((m, SB_WIDTH), bf16),
            jax.ShapeDtypeStruct((m, SB_WIDTH), bf16),
            jax.ShapeDtypeStruct((m, SB_WIDTH), bf16),
            jax.ShapeDtypeStruct((depth * m * SB_HEADS, HEAD_DIM), f32),
            jax.ShapeDtypeStruct((depth * m * SB_HEADS, HEAD_DIM), f32),
            jax.ShapeDtypeStruct((m, SGU_WIDTH), bf16),
            jax.ShapeDtypeStruct((depth * m, SGU_WIDTH), f32),
            jax.ShapeDtypeStruct((m, 2 * D_MODEL), bf16),
        ] + rider_shapes,
        input_output_aliases=aliases,
        scratch_shapes=[pltpu.VMEM((tm, D_MODEL), bf16)],
        compiler_params=pltpu.CompilerParams(
            dimension_semantics=("arbitrary", "arbitrary"), vmem_limit_bytes=VMEM_LIMIT),
        name="proj",
    )(*args, *to_cast)


def _suffix_matrix():
    i = jnp.arange(KEY_BLOCK)[:, None]
    jj = jnp.arange(KEY_BLOCK)[None, :]
    strict = (i > jj).astype(bf16)
    half = jnp.concatenate([strict, jnp.ones((KEY_BLOCK, KEY_BLOCK), bf16)], axis=1)
    return jnp.concatenate([half, half], axis=0)


def _sweep_step(chains, acc_ref, c_ref, tri, mask, first):
    zs = [lax.dot_general(q, kblk, (((1,), (1,)), ((), ())), preferred_element_type=f32)
          for q, kblk, _ in chains]
    log_betas, sums = [], []
    for z in zs:
        log_beta = jnp.minimum(z, 0.0) - jnp.log(1.0 + jnp.exp(-jnp.abs(z)))
        log_stay = log_beta - z
        if mask is not None:
            log_stay = jnp.where(mask, log_stay, 0.0)
        hi = log_stay.astype(bf16)
        lo = (log_stay - hi.astype(f32)).astype(bf16)
        log_betas.append(log_beta)
        sums.append(jnp.dot(jnp.concatenate([hi, lo], axis=1), tri, preferred_element_type=f32))
    worst = None
    contribs = []
    for i, (log_beta, s, (_, _, vblk)) in enumerate(zip(log_betas, sums, chains)):
        log_w = log_beta + s[:, :KEY_BLOCK]
        c = s[:, KEY_BLOCK:]
        if not first:
            carried = c_ref[i][...]
            log_w = log_w + carried
            c = c + carried
        c_ref[i][...] = c
        worst = c if worst is None else jnp.maximum(worst, c)
        w = jnp.exp(log_w)
        if mask is not None:
            w = jnp.where(mask, w, 0.0)
        contribs.append(jnp.dot(w.astype(bf16), vblk, preferred_element_type=f32))
    for i, contrib in enumerate(contribs):
        if first:
            acc_ref[i][...] = contrib
        else:
            acc_ref[i][...] += contrib
    return jnp.max(worst)


def _head_lanes(h):
    return slice(h * HEAD_DIM, (h + 1) * HEAD_DIM)


def _attn_prompt_kernel(q_ref, kb_ref, vb_ref, tri_ref, o_ref, *state):
    acc_ref, c_ref = state[:SB_HEADS], state[SB_HEADS:]
    qb = pl.program_id(1)
    tri = tri_ref[...]
    rows = lax.broadcasted_iota(jnp.int32, (KEY_BLOCK, KEY_BLOCK), 0)
    cols = lax.broadcasted_iota(jnp.int32, (KEY_BLOCK, KEY_BLOCK), 1)

    def chains(kblock):
        ks = pl.multiple_of(kblock * KEY_BLOCK, KEY_BLOCK)
        return [(q_ref[:, _head_lanes(h)], kb_ref[pl.ds(ks, KEY_BLOCK), _head_lanes(h)],
                 vb_ref[pl.ds(ks, KEY_BLOCK), _head_lanes(h)]) for h in range(SB_HEADS)]

    worst = _sweep_step(chains(qb), acc_ref, c_ref, tri, cols < rows, True)

    def more(state):
        kblock, worst = state
        return jnp.logical_and(kblock >= 0, worst > WEIGHT_CUTOFF)

    def step(state):
        kblock, _ = state
        return kblock - 1, _sweep_step(chains(kblock), acc_ref, c_ref, tri, None, False)

    lax.while_loop(more, step, (qb - 1, worst))
    for h in range(SB_HEADS):
        o_ref[:, _head_lanes(h)] = acc_ref[h][...].astype(bf16)


def _attn_prompt(q, k, v, tri, batch, seq):
    q3 = q.reshape(batch, seq, SB_WIDTH)
    k3 = k.reshape(batch, seq, SB_WIDTH)
    v3 = v.reshape(batch, seq, SB_WIDTH)
    qblock = pl.BlockSpec((None, KEY_BLOCK, SB_WIDTH), lambda b, i: (b, i, 0))
    whole = pl.BlockSpec((None, seq, SB_WIDTH), lambda b, i: (b, 0, 0))
    out = pl.pallas_call(
        _attn_prompt_kernel,
        grid=(batch, seq // KEY_BLOCK),
        in_specs=[qblock, whole, whole,
                  pl.BlockSpec((2 * KEY_BLOCK, 2 * KEY_BLOCK), lambda b, i: (0, 0))],
        out_specs=qblock,
        out_shape=jax.ShapeDtypeStruct((batch, seq, SB_WIDTH), bf16),
        scratch_shapes=[pltpu.VMEM((KEY_BLOCK, HEAD_DIM), f32)] * (2 * SB_HEADS),
        compiler_params=pltpu.CompilerParams(
            dimension_semantics=("arbitrary", "arbitrary"), vmem_limit_bytes=VMEM_LIMIT),
        name="attn_prompt",
    )(q3, k3, v3, tri)
    return out.reshape(batch * seq, SB_WIDTH)


def _attn_sample_kernel(q_ref, kn_ref, vn_ref, tk_ref, tv_ref, ck_hbm, cv_hbm, tri_ref, o_ref,
                        kpad_ref, vpad_ref, kold_ref, vold_ref, sem, *state, layer, n_cache):
    g = pl.program_id(0)
    streams, t_new = q_ref.shape[0], q_ref.shape[1]
    acc_ref, c_ref = state[:streams * SB_HEADS], state[streams * SB_HEADS:]
    n_tail = tk_ref.shape[1] // CACHE_BLOCK_ROWS
    tri = tri_ref[...]
    kpad_ref[...] = jnp.zeros(kpad_ref.shape, bf16)
    vpad_ref[...] = jnp.zeros(vpad_ref.shape, bf16)
    kpad_ref[:, 0:t_new, :] = kn_ref[...]
    vpad_ref[:, 0:t_new, :] = vn_ref[...]
    rows = lax.broadcasted_iota(jnp.int32, (t_new, KEY_BLOCK), 0)
    cols = lax.broadcasted_iota(jnp.int32, (t_new, KEY_BLOCK), 1)

    heads = [(s, h) for s in range(streams) for h in range(SB_HEADS)]
    new_chains = [(q_ref[s, :, _head_lanes(h)], kpad_ref[s, :, _head_lanes(h)], vpad_ref[s, :, _head_lanes(h)])
                  for s, h in heads]
    worst = _sweep_step(new_chains, acc_ref, c_ref, tri, cols < rows, True)

    def cache_chains(k_ref, v_ref, first_row):
        out = []
        for s, h in heads:
            head_rows = pl.ds(first_row + h, KEY_BLOCK, stride=SB_HEADS)
            out.append((q_ref[s, :, _head_lanes(h)], k_ref[s, head_rows, :].astype(bf16),
                        v_ref[s, head_rows, :].astype(bf16)))
        return out

    def more(state):
        left, worst = state
        return jnp.logical_and(left > 0, worst > WEIGHT_CUTOFF)

    def tail_step(state):
        left, _ = state
        first_row = pl.multiple_of((left - 1) * CACHE_BLOCK_ROWS, CACHE_BLOCK_ROWS)
        return left - 1, _sweep_step(cache_chains(tk_ref, tv_ref, first_row), acc_ref, c_ref, tri, None, False)

    _, worst = lax.while_loop(more, tail_step, (n_tail, worst))

    def old_copies(block):
        window = (layer, pl.ds(g * streams, streams), pl.ds(block * CACHE_BLOCK_ROWS, CACHE_BLOCK_ROWS))
        return (pltpu.make_async_copy(ck_hbm.at[window], kold_ref, sem.at[0]),
                pltpu.make_async_copy(cv_hbm.at[window], vold_ref, sem.at[1]))

    def old_step(state):
        left, _ = state
        copies = old_copies(left - 1)
        for cp in copies:
            cp.start()
        for cp in copies:
            cp.wait()
        return left - 1, _sweep_step(cache_chains(kold_ref, vold_ref, 0), acc_ref, c_ref, tri, None, False)

    lax.while_loop(more, old_step, (n_cache - n_tail, worst))
    for i, (s, h) in enumerate(heads):
        o_ref[s, :, _head_lanes(h)] = acc_ref[i][...].astype(bf16)


def _attn_sample(q, k, v, cache_k, cache_v, layer, tri, batch, t_new):
    past = cache_k.shape[2]
    assert t_new <= KEY_BLOCK and past % (KEY_BLOCK * TAIL_BLOCKS) == 0 and batch % SAMPLE_STREAMS == 0
    n_cache = past // KEY_BLOCK
    q3 = q.reshape(batch, t_new, SB_WIDTH)
    k3 = k.reshape(batch, t_new, SB_WIDTH)
    v3 = v.reshape(batch, t_new, SB_WIDTH)
    ck = cache_k.reshape(cache_k.shape[0], batch, past * SB_HEADS, HEAD_DIM)
    cv = cache_v.reshape(cache_v.shape[0], batch, past * SB_HEADS, HEAD_DIM)
    new = pl.BlockSpec((SAMPLE_STREAMS, t_new, SB_WIDTH), lambda g: (g, 0, 0))
    tail = pl.BlockSpec((None, SAMPLE_STREAMS, TAIL_BLOCKS * CACHE_BLOCK_ROWS, HEAD_DIM),
                        lambda g: (layer, g, n_cache // TAIL_BLOCKS - 1, 0))
    hbm = pl.BlockSpec(memory_space=pl.ANY)
    out = pl.pallas_call(
        functools.partial(_attn_sample_kernel, layer=layer, n_cache=n_cache),
        grid=(batch // SAMPLE_STREAMS,),
        in_specs=[new, new, new, tail, tail, hbm, hbm,
                  pl.BlockSpec((2 * KEY_BLOCK, 2 * KEY_BLOCK), lambda g: (0, 0))],
        out_specs=new,
        out_shape=jax.ShapeDtypeStruct((batch, t_new, SB_WIDTH), bf16),
        scratch_shapes=[
            pltpu.VMEM((SAMPLE_STREAMS, KEY_BLOCK, SB_WIDTH), bf16),
            pltpu.VMEM((SAMPLE_STREAMS, KEY_BLOCK, SB_WIDTH), bf16),
            pltpu.VMEM((SAMPLE_STREAMS, CACHE_BLOCK_ROWS, HEAD_DIM), f32),
            pltpu.VMEM((SAMPLE_STREAMS, CACHE_BLOCK_ROWS, HEAD_DIM), f32),
            pltpu.SemaphoreType.DMA((2,)),
        ] + [pltpu.VMEM((t_new, HEAD_DIM), f32)] * (2 * SAMPLE_STREAMS * SB_HEADS),
        compiler_params=pltpu.CompilerParams(
            dimension_semantics=("arbitrary",), vmem_limit_bytes=VMEM_LIMIT),
        name="attn_sample",
    )(q3, k3, v3, ck, cv, ck, cv, tri)
    return out.reshape(batch * t_new, SB_WIDTH)


def _merge_kernel(a_ref, u_ref, vn_ref, gate_ref, x_ref, ws_ref, bs_ref, wa_ref, wb_ref, wo_ref,
                  h_ref, bb_ref, *, chunk_len):
    tm = a_ref.shape[0]
    rows = lax.broadcasted_iota(jnp.int32, (chunk_len, chunk_len), 0) // CHUNK
    cols = lax.broadcasted_iota(jnp.int32, (chunk_len, chunk_len), 1) // CHUNK
    causal = cols <= rows
    for g in range(SGU_GROUPS):
        w_g = jnp.where(causal, ws_ref[g], 0.0).astype(bf16)
        b_g = bs_ref[g]
        lanes = slice(g * SGU_GROUP_DIM, (g + 1) * SGU_GROUP_DIM)
        for n in range(tm // chunk_len):
            r = slice(n * chunk_len, (n + 1) * chunk_len)
            s = jnp.dot(w_g, vn_ref[r, lanes].astype(bf16), preferred_element_type=f32) + b_g
            bb_ref[r, lanes] = (u_ref[r, lanes].astype(f32) * s).astype(bf16)
    ta = jnp.dot(a_ref[...], wa_ref[...], preferred_element_type=f32)
    tb = jnp.dot(bb_ref[...], wb_ref[...], preferred_element_type=f32)
    merged = (gate_ref[:, :D_MODEL].astype(f32) * ta + gate_ref[:, D_MODEL:].astype(f32) * tb).astype(bf16)
    h_ref[...] = x_ref[...] + jnp.dot(merged, wo_ref[...], preferred_element_type=f32)


def _merge(a, u, vn_all, gates, x, ws, bs, wa, wb, wo, layer, chunk_len, tm):
    m = x.shape[0]
    row = lambda i: (i, 0)
    layer_row = lambda i: (layer * (m // tm) + i, 0)
    const2 = lambda i: (0, 0)
    const3 = lambda i: (0, 0, 0)
    once = pl.Buffered(1)
    return pl.pallas_call(
        functools.partial(_merge_kernel, chunk_len=chunk_len),
        grid=(m // tm,),
        in_specs=[
            pl.BlockSpec((tm, SB_WIDTH), row),
            pl.BlockSpec((tm, SGU_WIDTH), row),
            pl.BlockSpec((tm, SGU_WIDTH), layer_row),
            pl.BlockSpec((tm, 2 * D_MODEL), row),
            pl.BlockSpec((tm, D_MODEL), row),
            pl.BlockSpec((SGU_GROUPS, chunk_len, chunk_len), const3, pipeline_mode=once),
            pl.BlockSpec((SGU_GROUPS, chunk_len, SGU_GROUP_DIM), const3, pipeline_mode=once),
            pl.BlockSpec((SB_WIDTH, D_MODEL), const2, pipeline_mode=once),
            pl.BlockSpec((SGU_WIDTH, D_MODEL), const2, pipeline_mode=once),
            pl.BlockSpec((D_MODEL, D_MODEL), const2, pipeline_mode=once),
        ],
        out_specs=pl.BlockSpec((tm, D_MODEL), row),
        out_shape=jax.ShapeDtypeStruct((m, D_MODEL), f32),
        scratch_shapes=[pltpu.VMEM((tm, SGU_WIDTH), bf16)],
        compiler_params=pltpu.CompilerParams(
            dimension_semantics=("arbitrary",), vmem_limit_bytes=VMEM_LIMIT),
        name="merge",
    )(a, u, vn_all, gates, x, ws, bs, wa, wb, wo)


def _ffn_kernel(*refs, n_rider, final_norm):
    h_ref, g_ref, wg_ref, wu_ref, wd_ref, gf_ref = refs[:6]
    rider_src = refs[6:6 + n_rider]
    o_ref = refs[6 + n_rider]
    rider_dst, hn_ref = refs[7 + n_rider:7 + 2 * n_rider], refs[7 + 2 * n_rider]
    j = pl.program_id(1)
    tm = h_ref.shape[0]

    def partial_sums(first):
        _run_cast_rider(rider_src, rider_dst)
        for r in range(0, tm, ROW_CHUNK):
            rows = slice(r, r + ROW_CHUNK)
            if first:
                hn_ref[rows, :] = _rmsnorm_rows(h_ref[rows, :], g_ref[...]).astype(bf16)
            hn = hn_ref[rows, :]
            gate = jnp.dot(hn, wg_ref[...], preferred_element_type=f32)
            up = jnp.dot(hn, wu_ref[...], preferred_element_type=f32)
            act = (jax.nn.silu(gate) * up).astype(bf16)
            down = jnp.dot(act, wd_ref[...], preferred_element_type=f32)
            o_ref[rows, :] = (h_ref[rows, :] if first else o_ref[rows, :]) + down

    pl.when(j == 0)(functools.partial(partial_sums, True))
    pl.when(j > 0)(functools.partial(partial_sums, False))

    if final_norm:
        @pl.when(j == pl.num_programs(1) - 1)
        def _():
            o_ref[...] = _rmsnorm_rows(o_ref[...], gf_ref[...])


def _ffn(h, g, w_gate_up, w_down, g_final, to_cast, cast_layer, final_norm, tm, tf):
    m = h.shape[0]
    nm, nf = m // tm, D_FF // tf
    row = lambda i, j: (i, 0)
    vec = pl.BlockSpec((1, D_MODEL), lambda i, j: (0, 0))
    rider_in, rider_out, rider_shapes = _cast_rider(to_cast, cast_layer, nm * nf, lambda i, j: i * nf + j)
    return pl.pallas_call(
        functools.partial(_ffn_kernel, n_rider=len(to_cast), final_norm=final_norm),
        grid=(nm, nf),
        in_specs=[
            pl.BlockSpec((tm, D_MODEL), row),
            vec,
            pl.BlockSpec((D_MODEL, tf), lambda i, j: (0, j)),
            pl.BlockSpec((D_MODEL, tf), lambda i, j: (0, j + nf)),
            pl.BlockSpec((tf, D_MODEL), lambda i, j: (j, 0)),
            vec,
        ] + rider_in,
        out_specs=[pl.BlockSpec((tm, D_MODEL), row)] + rider_out,
        out_shape=[jax.ShapeDtypeStruct((m, D_MODEL), f32)] + rider_shapes,
        scratch_shapes=[pltpu.VMEM((tm, D_MODEL), bf16)],
        compiler_params=pltpu.CompilerParams(
            dimension_semantics=("arbitrary", "arbitrary"), vmem_limit_bytes=VMEM_LIMIT),
        name="ffn",
    )(h, g, w_gate_up, w_gate_up, w_down, g_final, *to_cast)


LATE_WEIGHTS = ("w_branch_a", "w_branch_b", "w_out", "w_gate_up", "w_down")


def _trunk(x, cache_k, cache_v, p, half, tri):
    batch, t = x.shape[0], x.shape[1]
    m = batch * t
    depth = p["w_in"].shape[0]
    chunk_len = min(t, SGU_LEN)
    h = x.reshape(m, D_MODEL)
    tm = min(512, m)
    stacked = None
    if ("w_in", 0) not in half:
        half["w_in", 0] = p["w_in"][0].astype(bf16)
    for l in range(depth):
        missing = [n for n in LATE_WEIGHTS if (n, l) not in half]
        q, kb, vb, k_all, v_all, u, vn_all, gates, *cast = _proj(
            h, p["norm_mix"][l][None], half["w_in", l], p["b_gate"][l][None], p["sgu_norm"][l][None],
            stacked, [p[n] for n in missing], l, depth, tm=tm)
        half.update({(n, l): w for n, w in zip(missing, cast)})
        stacked = (k_all, v_all, vn_all)
        if cache_k is None:
            a = _attn_prompt(q, kb, vb, tri, batch, t)
        else:
            a = _attn_sample(q, kb, vb, cache_k, cache_v, l, tri, batch, t)
        bs = jnp.broadcast_to(p["b_spatial"][l][:, :chunk_len, None], (SGU_GROUPS, chunk_len, SGU_GROUP_DIM))
        h = _merge(a, u, vn_all, gates, h, p["w_spatial"][l][:, :chunk_len, :chunk_len], bs,
                   half["w_branch_a", l], half["w_branch_b", l], half["w_out", l], l, chunk_len, tm=min(256, m))
        next_in = l + 1 < depth and ("w_in", l + 1) not in half
        h, *cast = _ffn(h, p["norm_ffn"][l][None], half["w_gate_up", l], half["w_down", l], p["norm_final"][None],
                        [p["w_in"]] if next_in else [], l + 1, final_norm=(l == depth - 1),
                        tm=min(1024, m), tf=512)
        if next_in:
            half["w_in", l + 1] = cast[0]
    k_all, v_all, vn_all = stacked
    return (h.reshape(batch, t, D_MODEL), k_all.reshape(depth, batch, t, SB_HEADS, HEAD_DIM),
            v_all.reshape(depth, batch, t, SB_HEADS, HEAD_DIM), vn_all.reshape(depth, batch, t, SGU_WIDTH))


def kernel(x_prompt, x_sample, cache_k, cache_v, norm_mix, w_in, b_gate, sgu_norm, w_spatial, b_spatial,
           w_branch_a, w_branch_b, w_out, norm_ffn, w_gate_up, w_down, norm_final):
    p = dict(
        norm_mix=norm_mix, b_gate=b_gate, sgu_norm=sgu_norm, w_spatial=w_spatial, b_spatial=b_spatial,
        norm_ffn=norm_ffn, norm_final=norm_final, w_in=w_in, w_branch_a=w_branch_a, w_branch_b=w_branch_b,
        w_out=w_out, w_gate_up=w_gate_up, w_down=w_down,
    )
    tri = _suffix_matrix()
    half = {}
    y_prompt, k_prompt, v_prompt, _ = _trunk(x_prompt, None, None, p, half, tri)
    y_sample, k_sample, v_sample, sgu_v_sample = _trunk(x_sample, cache_k, cache_v, p, half, tri)
    return (y_prompt, y_sample, k_prompt, v_prompt, k_sample, v_sample, sgu_v_sample)
```

```python
import functools

import jax
import jax.numpy as jnp
from jax import lax
from jax.experimental import pallas as pl
from jax.experimental.pallas import tpu as pltpu

D_MODEL = 2048
HEAD_DIM = 128
SB_HEADS = 8
SB_WIDTH = SB_HEADS * HEAD_DIM
SGU_GROUPS = 8
SGU_GROUP_DIM = 128
SGU_WIDTH = SGU_GROUPS * SGU_GROUP_DIM
SGU_LEN = 128
CHUNK = 64
D_FF = 5632
EPS = 1e-6
KEY_BLOCK = 128
CACHE_BLOCK_ROWS = KEY_BLOCK * SB_HEADS
TAIL_BLOCKS = 2
SAMPLE_STREAMS = 2
WEIGHT_CUTOFF = -105.0
SEG = 1024
ROW_CHUNK = 256
BF16_SUBLANES = 16
N_PROJ_TILES = (3 * SB_WIDTH + 2 * SGU_WIDTH + 2 * D_MODEL) // SEG
GATE_TILE0 = (3 * SB_WIDTH + 2 * SGU_WIDTH) // SEG
VMEM_LIMIT = 56 * 1024 * 1024

bf16 = jnp.bfloat16
f32 = jnp.float32


def _rmsnorm_rows(x, g):
    return x * lax.rsqrt(jnp.mean(x * x, axis=-1, keepdims=True) + EPS) * g


def _store_head_rows(dst_ref, row0, acc):
    n = acc.shape[0]
    for h in range(SB_HEADS):
        dst_ref[pl.ds(row0 * SB_HEADS + h, n, stride=SB_HEADS), :] = acc[:, h * HEAD_DIM:(h + 1) * HEAD_DIM]


def _slab_rows(rows, n_steps):
    for slab in range(BF16_SUBLANES, rows + 1, BF16_SUBLANES):
        if rows % slab == 0 and rows // slab <= n_steps:
            return slab
    raise ValueError(f"no slab size covers {rows} rows in {n_steps} steps")


def _cast_rider(weights, layer, n_steps, step_of):
    in_specs, out_specs, out_shapes = [], [], []
    for w in weights:
        _, rows, cols = w.shape
        slab = _slab_rows(rows, n_steps)
        last = rows // slab - 1
        in_specs.append(pl.BlockSpec(
            (None, slab, cols), lambda *ids, last=last: (layer, jnp.minimum(step_of(*ids), last), 0)))
        out_specs.append(pl.BlockSpec(
            (slab, cols), lambda *ids, last=last: (jnp.minimum(step_of(*ids), last), 0)))
        out_shapes.append(jax.ShapeDtypeStruct((rows, cols), bf16))
    return in_specs, out_specs, out_shapes


def _run_cast_rider(src_refs, dst_refs):
    for src, dst in zip(src_refs, dst_refs):
        dst[...] = src[...].astype(bf16)


def _proj_kernel(*refs, n_alias, n_rider):
    x_ref, g_ref, w_ref, bg_ref, sn_ref = refs[:5]
    rider_src = refs[5 + n_alias:5 + n_alias + n_rider]
    outs = refs[5 + n_alias + n_rider:]
    q_ref, kb_ref, vb_ref, k_ref, v_ref, u_ref, vn_ref, gate_ref = outs[:8]
    rider_dst, xn_ref = outs[8:8 + n_rider], outs[8 + n_rider]
    j = pl.program_id(1)
    tm = x_ref.shape[0]

    def segment(emit, normalise=False):
        _run_cast_rider(rider_src, rider_dst)
        for r in range(0, tm, ROW_CHUNK):
            rows = slice(r, r + ROW_CHUNK)
            if normalise:
                xn_ref[rows, :] = _rmsnorm_rows(x_ref[rows, :], g_ref[...]).astype(bf16)
            emit(r, rows, jnp.dot(xn_ref[rows, :], w_ref[...], preferred_element_type=f32))

    @pl.when(j == 0)
    def _():
        def emit(r, rows, acc):
            q_ref[rows, :] = (acc * (HEAD_DIM ** -0.5)).astype(bf16)
        segment(emit, normalise=True)

    @pl.when(j == 1)
    def _():
        def emit(r, rows, acc):
            kb_ref[rows, :] = acc.astype(bf16)
            _store_head_rows(k_ref, r, acc)
        segment(emit)

    @pl.when(j == 2)
    def _():
        def emit(r, rows, acc):
            vb_ref[rows, :] = acc.astype(bf16)
            _store_head_rows(v_ref, r, acc)
        segment(emit)

    @pl.when(j == 3)
    def _():
        def emit(r, rows, acc):
            u_ref[rows, :] = jax.nn.gelu(acc).astype(bf16)
        segment(emit)

    @pl.when(j == 4)
    def _():
        def emit(r, rows, acc):
            vn_ref[rows, :] = _rmsnorm_rows(jax.nn.gelu(acc), sn_ref[...])
        segment(emit)

    @pl.when(j >= GATE_TILE0)
    def _():
        def emit(r, rows, acc):
            gate_ref[rows, :] = jax.nn.sigmoid(acc + bg_ref[...]).astype(bf16)
        segment(emit)


def _proj(x, g, w, bg, sn, stacked, to_cast, layer, depth, tm):
    m = x.shape[0]
    nm = m // tm
    gate_col = lambda i, j: (i, jnp.maximum(j - GATE_TILE0, 0))
    row = lambda i, j: (i, 0)
    layer_row = lambda i, j: (layer * nm + i, 0)
    in_specs = [
        pl.BlockSpec((tm, D_MODEL), row),
        pl.BlockSpec((1, D_MODEL), lambda i, j: (0, 0)),
        pl.BlockSpec((D_MODEL, SEG), lambda i, j: (0, j)),
        pl.BlockSpec((1, SEG), lambda i, j: (0, jnp.maximum(j - GATE_TILE0, 0))),
        pl.BlockSpec((1, SEG), lambda i, j: (0, 0)),
    ]
    args = [x, g, w, bg, sn]
    aliases = {}
    if stacked is not None:
        in_specs += [pl.BlockSpec(memory_space=pl.ANY)] * 3
        args += list(stacked)
        aliases = {5: 3, 6: 4, 7: 6}
    rider_in, rider_out, rider_shapes = _cast_rider(
        to_cast, layer, nm * N_PROJ_TILES, lambda i, j: i * N_PROJ_TILES + j)
    return pl.pallas_call(
        functools.partial(_proj_kernel, n_alias=len(args) - 5, n_rider=len(to_cast)),
        grid=(nm, N_PROJ_TILES),
        in_specs=in_specs + rider_in,
        out_specs=[
            pl.BlockSpec((tm, SEG), row),
            pl.BlockSpec((tm, SEG), row),
            pl.BlockSpec((tm, SEG), row),
            pl.BlockSpec((tm * SB_HEADS, HEAD_DIM), layer_row),
            pl.BlockSpec((tm * SB_HEADS, HEAD_DIM), layer_row),
            pl.BlockSpec((tm, SEG), row),
            pl.BlockSpec((tm, SEG), layer_row),
            pl.BlockSpec((tm, SEG), gate_col),
        ] + rider_out,
        out_shape=[
            jax.ShapeDtypeStruct((m, SB_WIDTH), bf16),
            jax.ShapeDtypeStruct((m, SB_WIDTH), bf16),
            jax.ShapeDtypeStruct((m, SB_WIDTH), bf16),
            jax.ShapeDtypeStruct((depth * m * SB_HEADS, HEAD_DIM), f32),
            jax.ShapeDtypeStruct((depth * m * SB_HEADS, HEAD_DIM), f32),
            jax.ShapeDtypeStruct((m, SGU_WIDTH), bf16),
            jax.ShapeDtypeStruct((depth * m, SGU_WIDTH), f32),
            jax.ShapeDtypeStruct((m, 2 * D_MODEL), bf16),
        ] + rider_shapes,
        input_output_aliases=aliases,
        scratch_shapes=[pltpu.VMEM((tm, D_MODEL), bf16)],
        compiler_params=pltpu.CompilerParams(
            dimension_semantics=("arbitrary", "arbitrary"), vmem_limit_bytes=VMEM_LIMIT),
        name="proj",
    )(*args, *to_cast)


def _suffix_matrix():
    i = jnp.arange(KEY_BLOCK)[:, None]
    jj = jnp.arange(KEY_BLOCK)[None, :]
    strict = (i > jj).astype(bf16)
    half = jnp.concatenate([strict, jnp.ones((KEY_BLOCK, KEY_BLOCK), bf16)], axis=1)
    return jnp.concatenate([half, half], axis=0)


def _sweep_step(chains, acc_ref, c_ref, tri, first):
    zs = [lax.dot_general(q, kblk, (((1,), (1,)), ((), ())), preferred_element_type=f32)
          for _, q, kblk, _, _, _ in chains]
    log_betas, sums = [], []
    for z, (_, _, _, _, mask, _) in zip(zs, chains):
        log_beta = jnp.minimum(z, 0.0) - jnp.log(1.0 + jnp.exp(-jnp.abs(z)))
        log_stay = log_beta - z
        if mask is not None:
            log_stay = jnp.where(mask, log_stay, 0.0)
        hi = log_stay.astype(bf16)
        lo = (log_stay - hi.astype(f32)).astype(bf16)
        log_betas.append(log_beta)
        sums.append(jnp.dot(jnp.concatenate([hi, lo], axis=1), tri, preferred_element_type=f32))
    totals, contribs = [], {}
    for log_beta, s, (slot, _, _, vblk, mask, after) in zip(log_betas, sums, chains):
        log_w = log_beta + s[:, :KEY_BLOCK]
        c = s[:, KEY_BLOCK:]
        later = totals[after] if after is not None else (None if first else c_ref[slot][...])
        if later is not None:
            log_w = log_w + later
            c = c + later
        totals.append(c)
        w = jnp.exp(log_w)
        if mask is not None:
            w = jnp.where(mask, w, 0.0)
        contrib = jnp.dot(w.astype(bf16), vblk, preferred_element_type=f32)
        contribs[slot] = contrib if slot not in contribs else contribs[slot] + contrib
    oldest = {slot: c for c, (slot, *_) in zip(totals, chains)}
    worst = None
    for slot, c in oldest.items():
        c_ref[slot][...] = c
        worst = c if worst is None else jnp.maximum(worst, c)
    for slot, contrib in contribs.items():
        if first:
            acc_ref[slot][...] = contrib
        else:
            acc_ref[slot][...] += contrib
    return jnp.max(worst)


def _head_lanes(h):
    return slice(h * HEAD_DIM, (h + 1) * HEAD_DIM)


def _attn_prompt_kernel(*refs, n_rider):
    q_ref, kb_ref, vb_ref, tri_ref = refs[:4]
    rider_src, o_ref = refs[4:4 + n_rider], refs[4 + n_rider]
    rider_dst, state = refs[5 + n_rider:5 + 2 * n_rider], refs[5 + 2 * n_rider:]
    acc_ref, c_ref = state[:SB_HEADS], state[SB_HEADS:]
    qb = pl.program_id(1)
    tri = tri_ref[...]
    rows = lax.broadcasted_iota(jnp.int32, (KEY_BLOCK, KEY_BLOCK), 0)
    cols = lax.broadcasted_iota(jnp.int32, (KEY_BLOCK, KEY_BLOCK), 1)
    causal = cols < rows

    def chains(kblock, mask, after_offset=None):
        ks = pl.multiple_of(kblock * KEY_BLOCK, KEY_BLOCK)
        return [(h, q_ref[:, _head_lanes(h)], kb_ref[pl.ds(ks, KEY_BLOCK), _head_lanes(h)],
                 vb_ref[pl.ds(ks, KEY_BLOCK), _head_lanes(h)], mask,
                 None if after_offset is None else after_offset + h) for h in range(SB_HEADS)]

    def write_out():
        for h in range(SB_HEADS):
            o_ref[:, _head_lanes(h)] = acc_ref[h][...].astype(bf16)

    @pl.when(qb == 0)
    def _():
        _run_cast_rider(rider_src, rider_dst)
        _sweep_step(chains(qb, causal), acc_ref, c_ref, tri, True)
        write_out()

    @pl.when(qb > 0)
    def _():
        _run_cast_rider(rider_src, rider_dst)
        worst = _sweep_step(chains(qb, causal) + chains(qb - 1, None, after_offset=0), acc_ref, c_ref, tri, True)

        def more(state):
            kblock, worst = state
            return jnp.logical_and(kblock >= 0, worst > WEIGHT_CUTOFF)

        def step(state):
            kblock, _ = state
            return kblock - 1, _sweep_step(chains(kblock, None), acc_ref, c_ref, tri, False)

        lax.while_loop(more, step, (qb - 2, worst))
        write_out()


def _attn_prompt(q, k, v, tri, to_cast, layer, batch, seq):
    q3 = q.reshape(batch, seq, SB_WIDTH)
    k3 = k.reshape(batch, seq, SB_WIDTH)
    v3 = v.reshape(batch, seq, SB_WIDTH)
    nq = seq // KEY_BLOCK
    qblock = pl.BlockSpec((None, KEY_BLOCK, SB_WIDTH), lambda b, i: (b, i, 0))
    whole = pl.BlockSpec((None, seq, SB_WIDTH), lambda b, i: (b, 0, 0))
    rider_in, rider_out, rider_shapes = _cast_rider(to_cast, layer, batch * nq, lambda b, i: b * nq + i)
    out, *cast = pl.pallas_call(
        functools.partial(_attn_prompt_kernel, n_rider=len(to_cast)),
        grid=(batch, nq),
        in_specs=[qblock, whole, whole,
                  pl.BlockSpec((2 * KEY_BLOCK, 2 * KEY_BLOCK), lambda b, i: (0, 0))] + rider_in,
        out_specs=[qblock] + rider_out,
        out_shape=[jax.ShapeDtypeStruct((batch, seq, SB_WIDTH), bf16)] + rider_shapes,
        scratch_shapes=[pltpu.VMEM((KEY_BLOCK, HEAD_DIM), f32)] * (2 * SB_HEADS),
        compiler_params=pltpu.CompilerParams(
            dimension_semantics=("arbitrary", "arbitrary"), vmem_limit_bytes=VMEM_LIMIT),
        name="attn_prompt",
    )(q3, k3, v3, tri, *to_cast)
    return (out.reshape(batch * seq, SB_WIDTH), *cast)


def _attn_sample_kernel(q_ref, kn_ref, vn_ref, tk_ref, tv_ref, ck_hbm, cv_hbm, tri_ref, o_ref,
                        kpad_ref, vpad_ref, kold_ref, vold_ref, sem, *state, layer, n_cache):
    g = pl.program_id(0)
    streams, t_new = q_ref.shape[0], q_ref.shape[1]
    acc_ref, c_ref = state[:streams * SB_HEADS], state[streams * SB_HEADS:]
    n_tail = tk_ref.shape[1] // CACHE_BLOCK_ROWS
    tri = tri_ref[...]
    kpad_ref[...] = jnp.zeros(kpad_ref.shape, bf16)
    vpad_ref[...] = jnp.zeros(vpad_ref.shape, bf16)
    kpad_ref[:, 0:t_new, :] = kn_ref[...]
    vpad_ref[:, 0:t_new, :] = vn_ref[...]
    rows = lax.broadcasted_iota(jnp.int32, (t_new, KEY_BLOCK), 0)
    cols = lax.broadcasted_iota(jnp.int32, (t_new, KEY_BLOCK), 1)

    heads = [(s, h) for s in range(streams) for h in range(SB_HEADS)]
    causal = cols < rows
    new_chains = [(i, q_ref[s, :, _head_lanes(h)], kpad_ref[s, :, _head_lanes(h)], vpad_ref[s, :, _head_lanes(h)],
                   causal, None) for i, (s, h) in enumerate(heads)]
    worst = _sweep_step(new_chains, acc_ref, c_ref, tri, True)

    def cache_chains(k_ref, v_ref, first_row):
        out = []
        for i, (s, h) in enumerate(heads):
            head_rows = pl.ds(first_row + h, KEY_BLOCK, stride=SB_HEADS)
            out.append((i, q_ref[s, :, _head_lanes(h)], k_ref[s, head_rows, :].astype(bf16),
                        v_ref[s, head_rows, :].astype(bf16), None, None))
        return out

    def more(state):
        left, worst = state
        return jnp.logical_and(left > 0, worst > WEIGHT_CUTOFF)

    def tail_step(state):
        left, _ = state
        first_row = pl.multiple_of((left - 1) * CACHE_BLOCK_ROWS, CACHE_BLOCK_ROWS)
        return left - 1, _sweep_step(cache_chains(tk_ref, tv_ref, first_row), acc_ref, c_ref, tri, False)

    _, worst = lax.while_loop(more, tail_step, (n_tail, worst))

    def old_copies(block):
        window = (layer, pl.ds(g * streams, streams), pl.ds(block * CACHE_BLOCK_ROWS, CACHE_BLOCK_ROWS))
        return (pltpu.make_async_copy(ck_hbm.at[window], kold_ref, sem.at[0]),
                pltpu.make_async_copy(cv_hbm.at[window], vold_ref, sem.at[1]))

    def old_step(state):
        left, _ = state
        copies = old_copies(left - 1)
        for cp in copies:
            cp.start()
        for cp in copies:
            cp.wait()
        return left - 1, _sweep_step(cache_chains(kold_ref, vold_ref, 0), acc_ref, c_ref, tri, False)

    lax.while_loop(more, old_step, (n_cache - n_tail, worst))
    for i, (s, h) in enumerate(heads):
        o_ref[s, :, _head_lanes(h)] = acc_ref[i][...].astype(bf16)


def _attn_sample(q, k, v, cache_k, cache_v, layer, tri, batch, t_new):
    past = cache_k.shape[2]
    assert t_new <= KEY_BLOCK and past % (KEY_BLOCK * TAIL_BLOCKS) == 0 and batch % SAMPLE_STREAMS == 0
    n_cache = past // KEY_BLOCK
    q3 = q.reshape(batch, t_new, SB_WIDTH)
    k3 = k.reshape(batch, t_new, SB_WIDTH)
    v3 = v.reshape(batch, t_new, SB_WIDTH)
    ck = cache_k.reshape(cache_k.shape[0], batch, past * SB_HEADS, HEAD_DIM)
    cv = cache_v.reshape(cache_v.shape[0], batch, past * SB_HEADS, HEAD_DIM)
    new = pl.BlockSpec((SAMPLE_STREAMS, t_new, SB_WIDTH), lambda g: (g, 0, 0))
    tail = pl.BlockSpec((None, SAMPLE_STREAMS, TAIL_BLOCKS * CACHE_BLOCK_ROWS, HEAD_DIM),
                        lambda g: (layer, g, n_cache // TAIL_BLOCKS - 1, 0))
    hbm = pl.BlockSpec(memory_space=pl.ANY)
    out = pl.pallas_call(
        functools.partial(_attn_sample_kernel, layer=layer, n_cache=n_cache),
        grid=(batch // SAMPLE_STREAMS,),
        in_specs=[new, new, new, tail, tail, hbm, hbm,
                  pl.BlockSpec((2 * KEY_BLOCK, 2 * KEY_BLOCK), lambda g: (0, 0))],
        out_specs=new,
        out_shape=jax.ShapeDtypeStruct((batch, t_new, SB_WIDTH), bf16),
        scratch_shapes=[
            pltpu.VMEM((SAMPLE_STREAMS, KEY_BLOCK, SB_WIDTH), bf16),
            pltpu.VMEM((SAMPLE_STREAMS, KEY_BLOCK, SB_WIDTH), bf16),
            pltpu.VMEM((SAMPLE_STREAMS, CACHE_BLOCK_ROWS, HEAD_DIM), f32),
            pltpu.VMEM((SAMPLE_STREAMS, CACHE_BLOCK_ROWS, HEAD_DIM), f32),
            pltpu.SemaphoreType.DMA((2,)),
        ] + [pltpu.VMEM((t_new, HEAD_DIM), f32)] * (2 * SAMPLE_STREAMS * SB_HEADS),
        compiler_params=pltpu.CompilerParams(
            dimension_semantics=("arbitrary",), vmem_limit_bytes=VMEM_LIMIT),
        name="attn_sample",
    )(q3, k3, v3, ck, cv, ck, cv, tri)
    return out.reshape(batch * t_new, SB_WIDTH)


def _merge_kernel(a_ref, u_ref, vn_ref, gate_ref, x_ref, ws_ref, bs_ref, wa_ref, wb_ref, wo_ref,
                  h_ref, bb_ref, *, chunk_len):
    tm = a_ref.shape[0]
    rows = lax.broadcasted_iota(jnp.int32, (chunk_len, chunk_len), 0) // CHUNK
    cols = lax.broadcasted_iota(jnp.int32, (chunk_len, chunk_len), 1) // CHUNK
    causal = cols <= rows
    for g in range(SGU_GROUPS):
        w_g = jnp.where(causal, ws_ref[g], 0.0).astype(bf16)
        b_g = bs_ref[g]
        lanes = slice(g * SGU_GROUP_DIM, (g + 1) * SGU_GROUP_DIM)
        for n in range(tm // chunk_len):
            r = slice(n * chunk_len, (n + 1) * chunk_len)
            s = jnp.dot(w_g, vn_ref[r, lanes].astype(bf16), preferred_element_type=f32) + b_g
            bb_ref[r, lanes] = (u_ref[r, lanes].astype(f32) * s).astype(bf16)
    ta = jnp.dot(a_ref[...], wa_ref[...], preferred_element_type=f32)
    tb = jnp.dot(bb_ref[...], wb_ref[...], preferred_element_type=f32)
    merged = (gate_ref[:, :D_MODEL].astype(f32) * ta + gate_ref[:, D_MODEL:].astype(f32) * tb).astype(bf16)
    h_ref[...] = x_ref[...] + jnp.dot(merged, wo_ref[...], preferred_element_type=f32)


def _merge(a, u, vn_all, gates, x, ws, bs, wa, wb, wo, layer, chunk_len, tm):
    m = x.shape[0]
    row = lambda i: (i, 0)
    layer_row = lambda i: (layer * (m // tm) + i, 0)
    const2 = lambda i: (0, 0)
    const3 = lambda i: (0, 0, 0)
    once = pl.Buffered(1)
    return pl.pallas_call(
        functools.partial(_merge_kernel, chunk_len=chunk_len),
        grid=(m // tm,),
        in_specs=[
            pl.BlockSpec((tm, SB_WIDTH), row),
            pl.BlockSpec((tm, SGU_WIDTH), row),
            pl.BlockSpec((tm, SGU_WIDTH), layer_row),
            pl.BlockSpec((tm, 2 * D_MODEL), row),
            pl.BlockSpec((tm, D_MODEL), row),
            pl.BlockSpec((SGU_GROUPS, chunk_len, chunk_len), const3, pipeline_mode=once),
            pl.BlockSpec((SGU_GROUPS, chunk_len, SGU_GROUP_DIM), const3, pipeline_mode=once),
            pl.BlockSpec((SB_WIDTH, D_MODEL), const2, pipeline_mode=once),
            pl.BlockSpec((SGU_WIDTH, D_MODEL), const2, pipeline_mode=once),
            pl.BlockSpec((D_MODEL, D_MODEL), const2, pipeline_mode=once),
        ],
        out_specs=pl.BlockSpec((tm, D_MODEL), row),
        out_shape=jax.ShapeDtypeStruct((m, D_MODEL), f32),
        scratch_shapes=[pltpu.VMEM((tm, SGU_WIDTH), bf16)],
        compiler_params=pltpu.CompilerParams(
            dimension_semantics=("arbitrary",), vmem_limit_bytes=VMEM_LIMIT),
        name="merge",
    )(a, u, vn_all, gates, x, ws, bs, wa, wb, wo)


def _ffn_kernel(*refs, n_rider, final_norm):
    h_ref, g_ref, wg_ref, wu_ref, wd_ref, gf_ref = refs[:6]
    rider_src = refs[6:6 + n_rider]
    o_ref = refs[6 + n_rider]
    rider_dst, hn_ref = refs[7 + n_rider:7 + 2 * n_rider], refs[7 + 2 * n_rider]
    j = pl.program_id(1)
    tm = h_ref.shape[0]

    def partial_sums(first):
        _run_cast_rider(rider_src, rider_dst)
        for r in range(0, tm, ROW_CHUNK):
            rows = slice(r, r + ROW_CHUNK)
            if first:
                hn_ref[rows, :] = _rmsnorm_rows(h_ref[rows, :], g_ref[...]).astype(bf16)
            hn = hn_ref[rows, :]
            gate = jnp.dot(hn, wg_ref[...], preferred_element_type=f32)
            up = jnp.dot(hn, wu_ref[...], preferred_element_type=f32)
            act = (jax.nn.silu(gate) * up).astype(bf16)
            down = jnp.dot(act, wd_ref[...], preferred_element_type=f32)
            o_ref[rows, :] = (h_ref[rows, :] if first else o_ref[rows, :]) + down

    pl.when(j == 0)(functools.partial(partial_sums, True))
    pl.when(j > 0)(functools.partial(partial_sums, False))

    if final_norm:
        @pl.when(j == pl.num_programs(1) - 1)
        def _():
            o_ref[...] = _rmsnorm_rows(o_ref[...], gf_ref[...])


def _ffn(h, g, w_gate_up, w_down, g_final, to_cast, cast_layer, final_norm, tm, tf):
    m = h.shape[0]
    nm, nf = m // tm, D_FF // tf
    row = lambda i, j: (i, 0)
    vec = pl.BlockSpec((1, D_MODEL), lambda i, j: (0, 0))
    rider_in, rider_out, rider_shapes = _cast_rider(to_cast, cast_layer, nm * nf, lambda i, j: i * nf + j)
    return pl.pallas_call(
        functools.partial(_ffn_kernel, n_rider=len(to_cast), final_norm=final_norm),
        grid=(nm, nf),
        in_specs=[
            pl.BlockSpec((tm, D_MODEL), row),
            vec,
            pl.BlockSpec((D_MODEL, tf), lambda i, j: (0, j)),
            pl.BlockSpec((D_MODEL, tf), lambda i, j: (0, j + nf)),
            pl.BlockSpec((tf, D_MODEL), lambda i, j: (j, 0)),
            vec,
        ] + rider_in,
        out_specs=[pl.BlockSpec((tm, D_MODEL), row)] + rider_out,
        out_shape=[jax.ShapeDtypeStruct((m, D_MODEL), f32)] + rider_shapes,
        scratch_shapes=[pltpu.VMEM((tm, D_MODEL), bf16)],
        compiler_params=pltpu.CompilerParams(
            dimension_semantics=("arbitrary", "arbitrary"), vmem_limit_bytes=VMEM_LIMIT),
        name="ffn",
    )(h, g, w_gate_up, w_gate_up, w_down, g_final, *to_cast)


MERGE_WEIGHTS = ("w_branch_a", "w_branch_b", "w_out")
FFN_WEIGHTS = ("w_gate_up", "w_down")


def _trunk(x, cache_k, cache_v, p, half, tri):
    batch, t = x.shape[0], x.shape[1]
    m = batch * t
    depth = p["w_in"].shape[0]
    chunk_len = min(t, SGU_LEN)
    h = x.reshape(m, D_MODEL)
    tm = min(512, m)
    stacked = None
    if ("w_in", 0) not in half:
        half["w_in", 0] = p["w_in"][0].astype(bf16)
    for l in range(depth):
        missing = [n for n in MERGE_WEIGHTS if (n, l) not in half]
        q, kb, vb, k_all, v_all, u, vn_all, gates, *cast = _proj(
            h, p["norm_mix"][l][None], half["w_in", l], p["b_gate"][l][None], p["sgu_norm"][l][None],
            stacked, [p[n] for n in missing], l, depth, tm=tm)
        half.update({(n, l): w for n, w in zip(missing, cast)})
        stacked = (k_all, v_all, vn_all)
        missing = [n for n in FFN_WEIGHTS if (n, l) not in half]
        if cache_k is None:
            a, *cast = _attn_prompt(q, kb, vb, tri, [p[n] for n in missing], l, batch, t)
            half.update({(n, l): w for n, w in zip(missing, cast)})
        else:
            a = _attn_sample(q, kb, vb, cache_k, cache_v, l, tri, batch, t)
            half.update({(n, l): p[n][l].astype(bf16) for n in missing})
        bs = jnp.broadcast_to(p["b_spatial"][l][:, :chunk_len, None], (SGU_GROUPS, chunk_len, SGU_GROUP_DIM))
        h = _merge(a, u, vn_all, gates, h, p["w_spatial"][l][:, :chunk_len, :chunk_len], bs,
                   half["w_branch_a", l], half["w_branch_b", l], half["w_out", l], l, chunk_len, tm=min(256, m))
        next_in = l + 1 < depth and ("w_in", l + 1) not in half
        h, *cast = _ffn(h, p["norm_ffn"][l][None], half["w_gate_up", l], half["w_down", l], p["norm_final"][None],
                        [p["w_in"]] if next_in else [], l + 1, final_norm=(l == depth - 1),
                        tm=min(1024, m), tf=512)
        if next_in:
            half["w_in", l + 1] = cast[0]
    k_all, v_all, vn_all = stacked
    return (h.reshape(batch, t, D_MODEL), k_all.reshape(depth, batch, t, SB_HEADS, HEAD_DIM),
            v_all.reshape(depth, batch, t, SB_HEADS, HEAD_DIM), vn_all.reshape(depth, batch, t, SGU_WIDTH))


def kernel(x_prompt, x_sample, cache_k, cache_v, norm_mix, w_in, b_gate, sgu_norm, w_spatial, b_spatial,
           w_branch_a, w_branch_b, w_out, norm_ffn, w_gate_up, w_down, norm_final):
    p = dict(
        norm_mix=norm_mix, b_gate=b_gate, sgu_norm=sgu_norm, w_spatial=w_spatial, b_spatial=b_spatial,
        norm_ffn=norm_ffn, norm_final=norm_final, w_in=w_in, w_branch_a=w_branch_a, w_branch_b=w_branch_b,
        w_out=w_out, w_gate_up=w_gate_up, w_down=w_down,
    )
    tri = _suffix_matrix()
    half = {}
    y_prompt, k_prompt, v_prompt, _ = _trunk(x_prompt, None, None, p, half, tri)
    y_sample, k_sample, v_sample, sgu_v_sample = _trunk(x_sample, cache_k, cache_v, p, half, tri)
    return (y_prompt, y_sample, k_prompt, v_prompt, k_sample, v_sample, sgu_v_sample)
```

```python
import functools

import jax
import jax.numpy as jnp
from jax import lax
from jax.experimental import pallas as pl
from jax.experimental.pallas import tpu as pltpu

D_MODEL = 2048
HEAD_DIM = 128
SB_HEADS = 8
SB_WIDTH = SB_HEADS * HEAD_DIM
SGU_GROUPS = 8
SGU_GROUP_DIM = 128
SGU_WIDTH = SGU_GROUPS * SGU_GROUP_DIM
SGU_LEN = 128
CHUNK = 64
D_FF = 5632
EPS = 1e-6
KEY_BLOCK = 128
CACHE_BLOCK_ROWS = KEY_BLOCK * SB_HEADS
TAIL_BLOCKS = 2
SAMPLE_STREAMS = 2
WEIGHT_CUTOFF = -105.0
SEG = 1024
ROW_CHUNK = 256
BF16_SUBLANES = 16
N_PROJ_TILES = (3 * SB_WIDTH + 2 * SGU_WIDTH + 2 * D_MODEL) // SEG
GATE_TILE0 = (3 * SB_WIDTH + 2 * SGU_WIDTH) // SEG
SGU_V_TILE = GATE_TILE0 - 1
ACT_COLS = (N_PROJ_TILES - 1) * SEG
ACT_Q, ACT_K, ACT_V, ACT_U = 0, 1, 2, 3
PROJ_ROW_CHUNK = 512
VMEM_LIMIT = 56 * 1024 * 1024

bf16 = jnp.bfloat16
f32 = jnp.float32


def _rmsnorm_rows(x, g):
    return x * lax.rsqrt(jnp.mean(x * x, axis=-1, keepdims=True) + EPS) * g


def _store_head_rows(dst_ref, row0, acc):
    n = acc.shape[0]
    for h in range(SB_HEADS):
        dst_ref[pl.ds(row0 * SB_HEADS + h, n, stride=SB_HEADS), :] = acc[:, h * HEAD_DIM:(h + 1) * HEAD_DIM]


def _slab_rows(rows, n_steps):
    for slab in range(BF16_SUBLANES, rows + 1, BF16_SUBLANES):
        if rows % slab == 0 and rows // slab <= n_steps:
            return slab
    raise ValueError(f"no slab size covers {rows} rows in {n_steps} steps")


def _cast_rider(weights, layer, n_steps, step_of):
    in_specs, out_specs, out_shapes = [], [], []
    for w in weights:
        _, rows, cols = w.shape
        slab = _slab_rows(rows, n_steps)
        last = rows // slab - 1
        in_specs.append(pl.BlockSpec(
            (None, slab, cols), lambda *ids, last=last: (layer, jnp.minimum(step_of(*ids), last), 0)))
        out_specs.append(pl.BlockSpec(
            (slab, cols), lambda *ids, last=last: (jnp.minimum(step_of(*ids), last), 0)))
        out_shapes.append(jax.ShapeDtypeStruct((rows, cols), bf16))
    return in_specs, out_specs, out_shapes


def _run_cast_rider(src_refs, dst_refs):
    for src, dst in zip(src_refs, dst_refs):
        dst[...] = src[...].astype(bf16)


def _proj_kernel(*refs, n_alias, n_rider):
    x_ref, g_ref, w_ref, bg_ref, sn_ref = refs[:5]
    rider_src = refs[5 + n_alias:5 + n_alias + n_rider]
    outs = refs[5 + n_alias + n_rider:]
    act_ref, k_ref, v_ref, vn_ref = outs[:4]
    rider_dst, xn_ref = outs[4:4 + n_rider], outs[4 + n_rider]
    j = pl.program_id(1)
    tm = x_ref.shape[0]
    chunk = min(tm, PROJ_ROW_CHUNK)

    def segment(emit, normalise=False):
        _run_cast_rider(rider_src, rider_dst)
        for r in range(0, tm, chunk):
            rows = slice(r, r + chunk)
            if normalise:
                xn_ref[rows, :] = _rmsnorm_rows(x_ref[rows, :], g_ref[...]).astype(bf16)
            emit(r, rows, jnp.dot(xn_ref[rows, :], w_ref[...], preferred_element_type=f32))

    @pl.when(j == 0)
    def _():
        def emit(r, rows, acc):
            act_ref[rows, :] = (acc * (HEAD_DIM ** -0.5)).astype(bf16)
        segment(emit, normalise=True)

    @pl.when(j == 1)
    def _():
        def emit(r, rows, acc):
            act_ref[rows, :] = acc.astype(bf16)
            _store_head_rows(k_ref, r, acc)
        segment(emit)

    @pl.when(j == 2)
    def _():
        def emit(r, rows, acc):
            act_ref[rows, :] = acc.astype(bf16)
            _store_head_rows(v_ref, r, acc)
        segment(emit)

    @pl.when(j == 3)
    def _():
        def emit(r, rows, acc):
            act_ref[rows, :] = jax.nn.gelu(acc).astype(bf16)
        segment(emit)

    @pl.when(j == SGU_V_TILE)
    def _():
        def emit(r, rows, acc):
            vn_ref[rows, :] = _rmsnorm_rows(jax.nn.gelu(acc), sn_ref[...])
        segment(emit)

    @pl.when(j >= GATE_TILE0)
    def _():
        def emit(r, rows, acc):
            act_ref[rows, :] = jax.nn.sigmoid(acc + bg_ref[...]).astype(bf16)
        segment(emit)


def _proj(x, g, w, bg, sn, stacked, to_cast, layer, depth, tm):
    m = x.shape[0]
    nm = m // tm
    act_col = lambda i, j: (i, j - (j >= SGU_V_TILE).astype(jnp.int32))
    row = lambda i, j: (i, 0)
    layer_row = lambda i, j: (layer * nm + i, 0)
    in_specs = [
        pl.BlockSpec((tm, D_MODEL), row, pipeline_mode=pl.Buffered(1)),
        pl.BlockSpec((1, D_MODEL), lambda i, j: (0, 0)),
        pl.BlockSpec((D_MODEL, SEG), lambda i, j: (0, j)),
        pl.BlockSpec((1, SEG), lambda i, j: (0, jnp.maximum(j - GATE_TILE0, 0))),
        pl.BlockSpec((1, SEG), lambda i, j: (0, 0)),
    ]
    args = [x, g, w, bg, sn]
    aliases = {}
    if stacked is not None:
        in_specs += [pl.BlockSpec(memory_space=pl.ANY)] * 3
        args += list(stacked)
        aliases = {5: 1, 6: 2, 7: 3}
    rider_in, rider_out, rider_shapes = _cast_rider(
        to_cast, layer, nm * N_PROJ_TILES, lambda i, j: i * N_PROJ_TILES + j)
    return pl.pallas_call(
        functools.partial(_proj_kernel, n_alias=len(args) - 5, n_rider=len(to_cast)),
        grid=(nm, N_PROJ_TILES),
        in_specs=in_specs + rider_in,
        out_specs=[
            pl.BlockSpec((tm, SEG), act_col),
            pl.BlockSpec((tm * SB_HEADS, HEAD_DIM), layer_row),
            pl.BlockSpec((tm * SB_HEADS, HEAD_DIM), layer_row),
            pl.BlockSpec((tm, SEG), layer_row),
        ] + rider_out,
        out_shape=[
            jax.ShapeDtypeStruct((m, ACT_COLS), bf16),
            jax.ShapeDtypeStruct((depth * m * SB_HEADS, HEAD_DIM), f32),
            jax.ShapeDtypeStruct((depth * m * SB_HEADS, HEAD_DIM), f32),
            jax.ShapeDtypeStruct((depth * m, SGU_WIDTH), f32),
        ] + rider_shapes,
        input_output_aliases=aliases,
        scratch_shapes=[pltpu.VMEM((tm, D_MODEL), bf16)],
        compiler_params=pltpu.CompilerParams(
            dimension_semantics=("arbitrary", "arbitrary"), vmem_limit_bytes=VMEM_LIMIT),
        name="proj",
    )(*args, *to_cast)


def _suffix_matrix():
    i = jnp.arange(KEY_BLOCK)[:, None]
    jj = jnp.arange(KEY_BLOCK)[None, :]
    strict = (i > jj).astype(bf16)
    half = jnp.concatenate([strict, jnp.ones((KEY_BLOCK, KEY_BLOCK), bf16)], axis=1)
    return jnp.concatenate([half, half], axis=0)


def _sweep_step(chains, acc_ref, c_ref, tri, first):
    zs = [lax.dot_general(q, kblk, (((1,), (1,)), ((), ())), preferred_element_type=f32)
          for _, q, kblk, _, _, _ in chains]
    log_betas, sums = [], []
    for z, (_, _, _, _, mask, _) in zip(zs, chains):
        log_beta = jnp.minimum(z, 0.0) - jnp.log(1.0 + jnp.exp(-jnp.abs(z)))
        log_stay = log_beta - z
        if mask is not None:
            log_stay = jnp.where(mask, log_stay, 0.0)
        hi = log_stay.astype(bf16)
        lo = (log_stay - hi.astype(f32)).astype(bf16)
        log_betas.append(log_beta)
        sums.append(jnp.dot(jnp.concatenate([hi, lo], axis=1), tri, preferred_element_type=f32))
    totals, contribs = [], {}
    for log_beta, s, (slot, _, _, vblk, mask, after) in zip(log_betas, sums, chains):
        log_w = log_beta + s[:, :KEY_BLOCK]
        c = s[:, KEY_BLOCK:]
        later = totals[after] if after is not None else (None if first else c_ref[slot][...])
        if later is not None:
            log_w = log_w + later
            c = c + later
        totals.append(c)
        w = jnp.exp(log_w)
        if mask is not None:
            w = jnp.where(mask, w, 0.0)
        contrib = jnp.dot(w.astype(bf16), vblk, preferred_element_type=f32)
        contribs[slot] = contrib if slot not in contribs else contribs[slot] + contrib
    oldest = {slot: c for c, (slot, *_) in zip(totals, chains)}
    worst = None
    for slot, c in oldest.items():
        c_ref[slot][...] = c
        worst = c if worst is None else jnp.maximum(worst, c)
    for slot, contrib in contribs.items():
        if first:
            acc_ref[slot][...] = contrib
        else:
            acc_ref[slot][...] += contrib
    return jnp.max(worst)


def _head_lanes(h):
    return slice(h * HEAD_DIM, (h + 1) * HEAD_DIM)


def _attn_prompt_kernel(*refs, n_rider):
    q_ref, kb_ref, vb_ref, tri_ref = refs[:4]
    rider_src, o_ref = refs[4:4 + n_rider], refs[4 + n_rider]
    rider_dst, state = refs[5 + n_rider:5 + 2 * n_rider], refs[5 + 2 * n_rider:]
    acc_ref, c_ref = state[:SB_HEADS], state[SB_HEADS:]
    qb = pl.program_id(1)
    tri = tri_ref[...]
    rows = lax.broadcasted_iota(jnp.int32, (KEY_BLOCK, KEY_BLOCK), 0)
    cols = lax.broadcasted_iota(jnp.int32, (KEY_BLOCK, KEY_BLOCK), 1)
    causal = cols < rows

    def chains(kblock, mask, after_offset=None):
        ks = pl.multiple_of(kblock * KEY_BLOCK, KEY_BLOCK)
        return [(h, q_ref[:, _head_lanes(h)], kb_ref[pl.ds(ks, KEY_BLOCK), _head_lanes(h)],
                 vb_ref[pl.ds(ks, KEY_BLOCK), _head_lanes(h)], mask,
                 None if after_offset is None else after_offset + h) for h in range(SB_HEADS)]

    def write_out():
        for h in range(SB_HEADS):
            o_ref[:, _head_lanes(h)] = acc_ref[h][...].astype(bf16)

    @pl.when(qb == 0)
    def _():
        _run_cast_rider(rider_src, rider_dst)
        _sweep_step(chains(qb, causal), acc_ref, c_ref, tri, True)
        write_out()

    @pl.when(qb > 0)
    def _():
        _run_cast_rider(rider_src, rider_dst)
        worst = _sweep_step(chains(qb, causal) + chains(qb - 1, None, after_offset=0), acc_ref, c_ref, tri, True)

        def more(state):
            kblock, worst = state
            return jnp.logical_and(kblock >= 0, worst > WEIGHT_CUTOFF)

        def step(state):
            kblock, _ = state
            return kblock - 1, _sweep_step(chains(kblock, None), acc_ref, c_ref, tri, False)

        lax.while_loop(more, step, (qb - 2, worst))
        write_out()


def _attn_prompt(act, tri, to_cast, layer, batch, seq):
    act3 = act.reshape(batch, seq, ACT_COLS)
    nq = seq // KEY_BLOCK
    qblock = pl.BlockSpec((None, KEY_BLOCK, SB_WIDTH), lambda b, i: (b, i, ACT_Q))
    whole_k = pl.BlockSpec((None, seq, SB_WIDTH), lambda b, i: (b, 0, ACT_K))
    whole_v = pl.BlockSpec((None, seq, SB_WIDTH), lambda b, i: (b, 0, ACT_V))
    rider_in, rider_out, rider_shapes = _cast_rider(to_cast, layer, batch * nq, lambda b, i: b * nq + i)
    out, *cast = pl.pallas_call(
        functools.partial(_attn_prompt_kernel, n_rider=len(to_cast)),
        grid=(batch, nq),
        in_specs=[qblock, whole_k, whole_v,
                  pl.BlockSpec((2 * KEY_BLOCK, 2 * KEY_BLOCK), lambda b, i: (0, 0))] + rider_in,
        out_specs=[pl.BlockSpec((None, KEY_BLOCK, SB_WIDTH), lambda b, i: (b, i, 0))] + rider_out,
        out_shape=[jax.ShapeDtypeStruct((batch, seq, SB_WIDTH), bf16)] + rider_shapes,
        scratch_shapes=[pltpu.VMEM((KEY_BLOCK, HEAD_DIM), f32)] * (2 * SB_HEADS),
        compiler_params=pltpu.CompilerParams(
            dimension_semantics=("arbitrary", "arbitrary"), vmem_limit_bytes=VMEM_LIMIT),
        name="attn_prompt",
    )(act3, act3, act3, tri, *to_cast)
    return (out.reshape(batch * seq, SB_WIDTH), *cast)


def _attn_sample_kernel(q_ref, kn_ref, vn_ref, tk_ref, tv_ref, ck_hbm, cv_hbm, tri_ref, o_ref,
                        kpad_ref, vpad_ref, kold_ref, vold_ref, sem, *state, layer, n_cache):
    g = pl.program_id(0)
    streams, t_new = q_ref.shape[0], q_ref.shape[1]
    acc_ref, c_ref = state[:streams * SB_HEADS], state[streams * SB_HEADS:]
    n_tail = tk_ref.shape[1] // CACHE_BLOCK_ROWS
    tri = tri_ref[...]
    kpad_ref[...] = jnp.zeros(kpad_ref.shape, bf16)
    vpad_ref[...] = jnp.zeros(vpad_ref.shape, bf16)
    kpad_ref[:, 0:t_new, :] = kn_ref[...]
    vpad_ref[:, 0:t_new, :] = vn_ref[...]
    rows = lax.broadcasted_iota(jnp.int32, (t_new, KEY_BLOCK), 0)
    cols = lax.broadcasted_iota(jnp.int32, (t_new, KEY_BLOCK), 1)

    heads = [(s, h) for s in range(streams) for h in range(SB_HEADS)]
    causal = cols < rows
    new_chains = [(i, q_ref[s, :, _head_lanes(h)], kpad_ref[s, :, _head_lanes(h)], vpad_ref[s, :, _head_lanes(h)],
                   causal, None) for i, (s, h) in enumerate(heads)]
    worst = _sweep_step(new_chains, acc_ref, c_ref, tri, True)

    def cache_chains(k_ref, v_ref, first_row):
        out = []
        for i, (s, h) in enumerate(heads):
            head_rows = pl.ds(first_row + h, KEY_BLOCK, stride=SB_HEADS)
            out.append((i, q_ref[s, :, _head_lanes(h)], k_ref[s, head_rows, :].astype(bf16),
                        v_ref[s, head_rows, :].astype(bf16), None, None))
        return out

    def more(state):
        left, worst = state
        return jnp.logical_and(left > 0, worst > WEIGHT_CUTOFF)

    def tail_step(state):
        left, _ = state
        first_row = pl.multiple_of((left - 1) * CACHE_BLOCK_ROWS, CACHE_BLOCK_ROWS)
        return left - 1, _sweep_step(cache_chains(tk_ref, tv_ref, first_row), acc_ref, c_ref, tri, False)

    _, worst = lax.while_loop(more, tail_step, (n_tail, worst))

    def old_copies(block):
        window = (layer, pl.ds(g * streams, streams), pl.ds(block * CACHE_BLOCK_ROWS, CACHE_BLOCK_ROWS))
        return (pltpu.make_async_copy(ck_hbm.at[window], kold_ref, sem.at[0]),
                pltpu.make_async_copy(cv_hbm.at[window], vold_ref, sem.at[1]))

    def old_step(state):
        left, _ = state
        copies = old_copies(left - 1)
        for cp in copies:
            cp.start()
        for cp in copies:
            cp.wait()
        return left - 1, _sweep_step(cache_chains(kold_ref, vold_ref, 0), acc_ref, c_ref, tri, False)

    lax.while_loop(more, old_step, (n_cache - n_tail, worst))
    for i, (s, h) in enumerate(heads):
        o_ref[s, :, _head_lanes(h)] = acc_ref[i][...].astype(bf16)


def _attn_sample(act, cache_k, cache_v, layer, tri, batch, t_new):
    past = cache_k.shape[2]
    assert t_new <= KEY_BLOCK and past % (KEY_BLOCK * TAIL_BLOCKS) == 0 and batch % SAMPLE_STREAMS == 0
    n_cache = past // KEY_BLOCK
    act3 = act.reshape(batch, t_new, ACT_COLS)
    ck = cache_k.reshape(cache_k.shape[0], batch, past * SB_HEADS, HEAD_DIM)
    cv = cache_v.reshape(cache_v.shape[0], batch, past * SB_HEADS, HEAD_DIM)
    new = pl.BlockSpec((SAMPLE_STREAMS, t_new, SB_WIDTH), lambda g: (g, 0, 0))
    new_q, new_k, new_v = (pl.BlockSpec((SAMPLE_STREAMS, t_new, SB_WIDTH), lambda g, c=c: (g, 0, c))
                           for c in (ACT_Q, ACT_K, ACT_V))
    tail = pl.BlockSpec((None, SAMPLE_STREAMS, TAIL_BLOCKS * CACHE_BLOCK_ROWS, HEAD_DIM),
                        lambda g: (layer, g, n_cache // TAIL_BLOCKS - 1, 0))
    hbm = pl.BlockSpec(memory_space=pl.ANY)
    out = pl.pallas_call(
        functools.partial(_attn_sample_kernel, layer=layer, n_cache=n_cache),
        grid=(batch // SAMPLE_STREAMS,),
        in_specs=[new_q, new_k, new_v, tail, tail, hbm, hbm,
                  pl.BlockSpec((2 * KEY_BLOCK, 2 * KEY_BLOCK), lambda g: (0, 0))],
        out_specs=new,
        out_shape=jax.ShapeDtypeStruct((batch, t_new, SB_WIDTH), bf16),
        scratch_shapes=[
            pltpu.VMEM((SAMPLE_STREAMS, KEY_BLOCK, SB_WIDTH), bf16),
            pltpu.VMEM((SAMPLE_STREAMS, KEY_BLOCK, SB_WIDTH), bf16),
            pltpu.VMEM((SAMPLE_STREAMS, CACHE_BLOCK_ROWS, HEAD_DIM), f32),
            pltpu.VMEM((SAMPLE_STREAMS, CACHE_BLOCK_ROWS, HEAD_DIM), f32),
            pltpu.SemaphoreType.DMA((2,)),
        ] + [pltpu.VMEM((t_new, HEAD_DIM), f32)] * (2 * SAMPLE_STREAMS * SB_HEADS),
        compiler_params=pltpu.CompilerParams(
            dimension_semantics=("arbitrary",), vmem_limit_bytes=VMEM_LIMIT),
        name="attn_sample",
    )(act3, act3, act3, ck, cv, ck, cv, tri)
    return out.reshape(batch * t_new, SB_WIDTH)


def _merge_kernel(a_ref, u_ref, vn_ref, gate_ref, x_ref, ws_ref, bs_ref, wa_ref, wb_ref, wo_ref,
                  h_ref, bb_ref, *, chunk_len):
    tm = a_ref.shape[0]
    rows = lax.broadcasted_iota(jnp.int32, (chunk_len, chunk_len), 0) // CHUNK
    cols = lax.broadcasted_iota(jnp.int32, (chunk_len, chunk_len), 1) // CHUNK
    causal = cols <= rows
    for g in range(SGU_GROUPS):
        w_g = jnp.where(causal, ws_ref[g], 0.0).astype(bf16)
        b_g = bs_ref[g]
        lanes = slice(g * SGU_GROUP_DIM, (g + 1) * SGU_GROUP_DIM)
        for n in range(tm // chunk_len):
            r = slice(n * chunk_len, (n + 1) * chunk_len)
            s = jnp.dot(w_g, vn_ref[r, lanes].astype(bf16), preferred_element_type=f32) + b_g
            bb_ref[r, lanes] = (u_ref[r, lanes].astype(f32) * s).astype(bf16)
    ta = jnp.dot(a_ref[...], wa_ref[...], preferred_element_type=f32)
    tb = jnp.dot(bb_ref[...], wb_ref[...], preferred_element_type=f32)
    merged = (gate_ref[:, :D_MODEL].astype(f32) * ta + gate_ref[:, D_MODEL:].astype(f32) * tb).astype(bf16)
    h_ref[...] = x_ref[...] + jnp.dot(merged, wo_ref[...], preferred_element_type=f32)


def _merge(a, act, vn_all, x, ws, bs, wa, wb, wo, layer, chunk_len, tm):
    m = x.shape[0]
    row = lambda i: (i, 0)
    layer_row = lambda i: (layer * (m // tm) + i, 0)
    const2 = lambda i: (0, 0)
    const3 = lambda i: (0, 0, 0)
    once = pl.Buffered(1)
    assert (ACT_U + 1) * SEG == 2 * D_MODEL
    return pl.pallas_call(
        functools.partial(_merge_kernel, chunk_len=chunk_len),
        grid=(m // tm,),
        in_specs=[
            pl.BlockSpec((tm, SB_WIDTH), row),
            pl.BlockSpec((tm, SGU_WIDTH), lambda i: (i, ACT_U)),
            pl.BlockSpec((tm, SGU_WIDTH), layer_row),
            pl.BlockSpec((tm, 2 * D_MODEL), lambda i: (i, 1)),
            pl.BlockSpec((tm, D_MODEL), row),
            pl.BlockSpec((SGU_GROUPS, chunk_len, chunk_len), const3, pipeline_mode=once),
            pl.BlockSpec((SGU_GROUPS, chunk_len, SGU_GROUP_DIM), const3, pipeline_mode=once),
            pl.BlockSpec((SB_WIDTH, D_MODEL), const2, pipeline_mode=once),
            pl.BlockSpec((SGU_WIDTH, D_MODEL), const2, pipeline_mode=once),
            pl.BlockSpec((D_MODEL, D_MODEL), const2, pipeline_mode=once),
        ],
        out_specs=pl.BlockSpec((tm, D_MODEL), row),
        out_shape=jax.ShapeDtypeStruct((m, D_MODEL), f32),
        scratch_shapes=[pltpu.VMEM((tm, SGU_WIDTH), bf16)],
        compiler_params=pltpu.CompilerParams(
            dimension_semantics=("arbitrary",), vmem_limit_bytes=VMEM_LIMIT),
        name="merge",
    )(a, act, vn_all, act, x, ws, bs, wa, wb, wo)


def _ffn_kernel(*refs, n_rider, final_norm):
    h_ref, g_ref, wg_ref, wu_ref, wd_ref, gf_ref = refs[:6]
    rider_src = refs[6:6 + n_rider]
    o_ref = refs[6 + n_rider]
    rider_dst, hn_ref = refs[7 + n_rider:7 + 2 * n_rider], refs[7 + 2 * n_rider]
    j = pl.program_id(1)
    tm = h_ref.shape[0]

    def partial_sums(first):
        _run_cast_rider(rider_src, rider_dst)
        for r in range(0, tm, ROW_CHUNK):
            rows = slice(r, r + ROW_CHUNK)
            if first:
                hn_ref[rows, :] = _rmsnorm_rows(h_ref[rows, :], g_ref[...]).astype(bf16)
            hn = hn_ref[rows, :]
            gate = jnp.dot(hn, wg_ref[...], preferred_element_type=f32)
            up = jnp.dot(hn, wu_ref[...], preferred_element_type=f32)
            act = (jax.nn.silu(gate) * up).astype(bf16)
            down = jnp.dot(act, wd_ref[...], preferred_element_type=f32)
            o_ref[rows, :] = (h_ref[rows, :] if first else o_ref[rows, :]) + down

    pl.when(j == 0)(functools.partial(partial_sums, True))
    pl.when(j > 0)(functools.partial(partial_sums, False))

    if final_norm:
        @pl.when(j == pl.num_programs(1) - 1)
        def _():
            o_ref[...] = _rmsnorm_rows(o_ref[...], gf_ref[...])


def _ffn(h, g, w_gate_up, w_down, g_final, to_cast, cast_layer, final_norm, tm, tf):
    m = h.shape[0]
    nm, nf = m // tm, D_FF // tf
    row = lambda i, j: (i, 0)
    vec = pl.BlockSpec((1, D_MODEL), lambda i, j: (0, 0))
    rider_in, rider_out, rider_shapes = _cast_rider(to_cast, cast_layer, nm * nf, lambda i, j: i * nf + j)
    return pl.pallas_call(
        functools.partial(_ffn_kernel, n_rider=len(to_cast), final_norm=final_norm),
        grid=(nm, nf),
        in_specs=[
            pl.BlockSpec((tm, D_MODEL), row),
            vec,
            pl.BlockSpec((D_MODEL, tf), lambda i, j: (0, j)),
            pl.BlockSpec((D_MODEL, tf), lambda i, j: (0, j + nf)),
            pl.BlockSpec((tf, D_MODEL), lambda i, j: (j, 0)),
            vec,
        ] + rider_in,
        out_specs=[pl.BlockSpec((tm, D_MODEL), row)] + rider_out,
        out_shape=[jax.ShapeDtypeStruct((m, D_MODEL), f32)] + rider_shapes,
        scratch_shapes=[pltpu.VMEM((tm, D_MODEL), bf16)],
        compiler_params=pltpu.CompilerParams(
            dimension_semantics=("arbitrary", "arbitrary"), vmem_limit_bytes=VMEM_LIMIT),
        name="ffn",
    )(h, g, w_gate_up, w_gate_up, w_down, g_final, *to_cast)


MERGE_WEIGHTS = ("w_branch_a", "w_branch_b", "w_out")
FFN_WEIGHTS = ("w_gate_up", "w_down")


def _trunk(x, cache_k, cache_v, p, half, tri):
    batch, t = x.shape[0], x.shape[1]
    m = batch * t
    depth = p["w_in"].shape[0]
    chunk_len = min(t, SGU_LEN)
    h = x.reshape(m, D_MODEL)
    stacked = None
    if ("w_in", 0) not in half:
        half["w_in", 0] = p["w_in"][0].astype(bf16)
    for l in range(depth):
        missing = [n for n in MERGE_WEIGHTS if (n, l) not in half]
        act, k_all, v_all, vn_all, *cast = _proj(
            h, p["norm_mix"][l][None], half["w_in", l], p["b_gate"][l][None], p["sgu_norm"][l][None],
            stacked, [p[n] for n in missing], l, depth, tm=min(1024, m))
        half.update({(n, l): w for n, w in zip(missing, cast)})
        stacked = (k_all, v_all, vn_all)
        missing = [n for n in FFN_WEIGHTS if (n, l) not in half]
        if cache_k is None:
            a, *cast = _attn_prompt(act, tri, [p[n] for n in missing], l, batch, t)
            half.update({(n, l): w for n, w in zip(missing, cast)})
        else:
            a = _attn_sample(act, cache_k, cache_v, l, tri, batch, t)
            half.update({(n, l): p[n][l].astype(bf16) for n in missing})
        bs = jnp.broadcast_to(p["b_spatial"][l][:, :chunk_len, None], (SGU_GROUPS, chunk_len, SGU_GROUP_DIM))
        h = _merge(a, act, vn_all, h, p["w_spatial"][l][:, :chunk_len, :chunk_len], bs,
                   half["w_branch_a", l], half["w_branch_b", l], half["w_out", l], l, chunk_len, tm=min(256, m))
        next_in = l + 1 < depth and ("w_in", l + 1) not in half
        h, *cast = _ffn(h, p["norm_ffn"][l][None], half["w_gate_up", l], half["w_down", l], p["norm_final"][None],
                        [p["w_in"]] if next_in else [], l + 1, final_norm=(l == depth - 1),
                        tm=min(1024, m), tf=512)
        if next_in:
            half["w_in", l + 1] = cast[0]
    k_all, v_all, vn_all = stacked
    return (h.reshape(batch, t, D_MODEL), k_all.reshape(depth, batch, t, SB_HEADS, HEAD_DIM),
            v_all.reshape(depth, batch, t, SB_HEADS, HEAD_DIM), vn_all.reshape(depth, batch, t, SGU_WIDTH))


def kernel(x_prompt, x_sample, cache_k, cache_v, norm_mix, w_in, b_gate, sgu_norm, w_spatial, b_spatial,
           w_branch_a, w_branch_b, w_out, norm_ffn, w_gate_up, w_down, norm_final):
    p = dict(
        norm_mix=norm_mix, b_gate=b_gate, sgu_norm=sgu_norm, w_spatial=w_spatial, b_spatial=b_spatial,
        norm_ffn=norm_ffn, norm_final=norm_final, w_in=w_in, w_branch_a=w_branch_a, w_branch_b=w_branch_b,
        w_out=w_out, w_gate_up=w_gate_up, w_down=w_down,
    )
    tri = _suffix_matrix()
    half = {}
    y_prompt, k_prompt, v_prompt, _ = _trunk(x_prompt, None, None, p, half, tri)
    y_sample, k_sample, v_sample, sgu_v_sample = _trunk(x_sample, cache_k, cache_v, p, half, tri)
    return (y_prompt, y_sample, k_prompt, v_prompt, k_sample, v_sample, sgu_v_sample)
```

```python
import functools

import jax
import jax.numpy as jnp
from jax import lax
from jax.experimental import pallas as pl
from jax.experimental.pallas import tpu as pltpu

D_MODEL = 2048
HEAD_DIM = 128
SB_HEADS = 8
SB_WIDTH = SB_HEADS * HEAD_DIM
SGU_GROUPS = 8
SGU_GROUP_DIM = 128
SGU_WIDTH = SGU_GROUPS * SGU_GROUP_DIM
SGU_LEN = 128
CHUNK = 64
D_FF = 5632
EPS = 1e-6
KEY_BLOCK = 128
CACHE_BLOCK_ROWS = KEY_BLOCK * SB_HEADS
TAIL_BLOCKS = 2
SAMPLE_STREAMS = 2
Q_SCALE = HEAD_DIM ** -0.5
WEIGHT_CUTOFF = -105.0
SEG = 1024
ROW_CHUNK = 256
BF16_SUBLANES = 16
N_PROJ_TILES = (3 * SB_WIDTH + 2 * SGU_WIDTH + 2 * D_MODEL) // SEG
GATE_TILE0 = (3 * SB_WIDTH + 2 * SGU_WIDTH) // SEG
SGU_V_TILE = GATE_TILE0 - 1
ACT_COLS = (N_PROJ_TILES - 1) * SEG
ACT_Q, ACT_K, ACT_V, ACT_U = 0, 1, 2, 3
PROJ_ROW_CHUNK = 512
VMEM_LIMIT = 56 * 1024 * 1024

bf16 = jnp.bfloat16
f32 = jnp.float32


def _rmsnorm_rows(x, g):
    return x * lax.rsqrt(jnp.mean(x * x, axis=-1, keepdims=True) + EPS) * g


def _store_head_rows(dst_ref, row0, acc):
    n = acc.shape[0]
    for h in range(SB_HEADS):
        dst_ref[pl.ds(row0 * SB_HEADS + h, n, stride=SB_HEADS), :] = acc[:, h * HEAD_DIM:(h + 1) * HEAD_DIM]


def _slab_rows(rows, n_steps):
    for slab in range(BF16_SUBLANES, rows + 1, BF16_SUBLANES):
        if rows % slab == 0 and rows // slab <= n_steps:
            return slab
    raise ValueError(f"no slab size covers {rows} rows in {n_steps} steps")


def _cast_rider(weights, layer, n_steps, step_of):
    in_specs, out_specs, out_shapes = [], [], []
    for w in weights:
        _, rows, cols = w.shape
        slab = _slab_rows(rows, n_steps)
        last = rows // slab - 1
        in_specs.append(pl.BlockSpec(
            (None, slab, cols), lambda *ids, last=last: (layer, jnp.minimum(step_of(*ids), last), 0)))
        out_specs.append(pl.BlockSpec(
            (slab, cols), lambda *ids, last=last: (jnp.minimum(step_of(*ids), last), 0)))
        out_shapes.append(jax.ShapeDtypeStruct((rows, cols), bf16))
    return in_specs, out_specs, out_shapes


def _run_cast_rider(src_refs, dst_refs):
    for src, dst in zip(src_refs, dst_refs):
        dst[...] = src[...].astype(bf16)


def _proj_kernel(*refs, n_alias, n_rider):
    x_hbm, g_ref, w_ref, bg_ref, sn_ref = refs[:5]
    rider_src = refs[5 + n_alias:5 + n_alias + n_rider]
    outs = refs[5 + n_alias + n_rider:]
    act_ref, k_ref, v_ref, vn_ref = outs[:4]
    rider_dst = outs[4:4 + n_rider]
    xn_ref, x_ref, x_sem = outs[4 + n_rider:]
    i, j = pl.program_id(0), pl.program_id(1)
    tm = x_ref.shape[0]
    chunk = min(tm, PROJ_ROW_CHUNK)

    def x_copy(tile):
        return pltpu.make_async_copy(x_hbm.at[pl.ds(tile * tm, tm), :], x_ref, x_sem.at[0])

    def segment(emit, normalise=False):
        _run_cast_rider(rider_src, rider_dst)
        for r in range(0, tm, chunk):
            rows = slice(r, r + chunk)
            if normalise:
                xn_ref[rows, :] = _rmsnorm_rows(x_ref[rows, :], g_ref[...]).astype(bf16)
            emit(r, rows, jnp.dot(xn_ref[rows, :], w_ref[...], preferred_element_type=f32))

    @pl.when(j == 0)
    def _():
        @pl.when(i == 0)
        def _():
            x_copy(0).start()
        x_copy(i).wait()

        def emit(r, rows, acc):
            act_ref[rows, :] = (acc * Q_SCALE).astype(bf16)
        segment(emit, normalise=True)

    @pl.when(j == 1)
    def _():
        @pl.when(i + 1 < pl.num_programs(0))
        def _():
            x_copy(i + 1).start()

        def emit(r, rows, acc):
            act_ref[rows, :] = acc.astype(bf16)
            _store_head_rows(k_ref, r, acc)
        segment(emit)

    @pl.when(j == 2)
    def _():
        def emit(r, rows, acc):
            act_ref[rows, :] = acc.astype(bf16)
            _store_head_rows(v_ref, r, acc)
        segment(emit)

    @pl.when(j == 3)
    def _():
        def emit(r, rows, acc):
            act_ref[rows, :] = jax.nn.gelu(acc).astype(bf16)
        segment(emit)

    @pl.when(j == SGU_V_TILE)
    def _():
        def emit(r, rows, acc):
            vn_ref[rows, :] = _rmsnorm_rows(jax.nn.gelu(acc), sn_ref[...])
        segment(emit)

    @pl.when(j >= GATE_TILE0)
    def _():
        def emit(r, rows, acc):
            act_ref[rows, :] = jax.nn.sigmoid(acc + bg_ref[...]).astype(bf16)
        segment(emit)


def _proj(x, g, w, bg, sn, stacked, to_cast, layer, depth, tm):
    m = x.shape[0]
    nm = m // tm
    act_col = lambda i, j: (i, j - (j >= SGU_V_TILE).astype(jnp.int32))
    row = lambda i, j: (i, 0)
    layer_row = lambda i, j: (layer * nm + i, 0)
    in_specs = [
        pl.BlockSpec(memory_space=pl.ANY),
        pl.BlockSpec((1, D_MODEL), lambda i, j: (0, 0)),
        pl.BlockSpec((D_MODEL, SEG), lambda i, j: (0, j)),
        pl.BlockSpec((1, SEG), lambda i, j: (0, jnp.maximum(j - GATE_TILE0, 0))),
        pl.BlockSpec((1, SEG), lambda i, j: (0, 0)),
    ]
    args = [x, g, w, bg, sn]
    aliases = {}
    if stacked is not None:
        in_specs += [pl.BlockSpec(memory_space=pl.ANY)] * 3
        args += list(stacked)
        aliases = {5: 1, 6: 2, 7: 3}
    rider_in, rider_out, rider_shapes = _cast_rider(
        to_cast, layer, nm * N_PROJ_TILES, lambda i, j: i * N_PROJ_TILES + j)
    return pl.pallas_call(
        functools.partial(_proj_kernel, n_alias=len(args) - 5, n_rider=len(to_cast)),
        grid=(nm, N_PROJ_TILES),
        in_specs=in_specs + rider_in,
        out_specs=[
            pl.BlockSpec((tm, SEG), act_col),
            pl.BlockSpec((tm * SB_HEADS, HEAD_DIM), layer_row),
            pl.BlockSpec((tm * SB_HEADS, HEAD_DIM), layer_row),
            pl.BlockSpec((tm, SEG), layer_row),
        ] + rider_out,
        out_shape=[
            jax.ShapeDtypeStruct((m, ACT_COLS), bf16),
            jax.ShapeDtypeStruct((depth * m * SB_HEADS, HEAD_DIM), f32),
            jax.ShapeDtypeStruct((depth * m * SB_HEADS, HEAD_DIM), f32),
            jax.ShapeDtypeStruct((depth * m, SGU_WIDTH), f32),
        ] + rider_shapes,
        input_output_aliases=aliases,
        scratch_shapes=[pltpu.VMEM((tm, D_MODEL), bf16), pltpu.VMEM((tm, D_MODEL), f32),
                        pltpu.SemaphoreType.DMA((1,))],
        compiler_params=pltpu.CompilerParams(
            dimension_semantics=("arbitrary", "arbitrary"), vmem_limit_bytes=VMEM_LIMIT),
        name="proj",
    )(*args, *to_cast)


def _suffix_matrix():
    i = jnp.arange(KEY_BLOCK)[:, None]
    jj = jnp.arange(KEY_BLOCK)[None, :]
    strict = (i > jj).astype(bf16)
    half = jnp.concatenate([strict, jnp.ones((KEY_BLOCK, KEY_BLOCK), bf16)], axis=1)
    return jnp.concatenate([half, half], axis=0)


def _sweep_step(chains, acc_ref, c_ref, tri, first):
    zs = [lax.dot_general(q, kblk, (((1,), (1,)), ((), ())), preferred_element_type=f32)
          for _, q, kblk, _, _, _ in chains]
    log_betas, sums = [], []
    for z, (_, _, _, _, mask, _) in zip(zs, chains):
        log_beta = jnp.minimum(z, 0.0) - jnp.log(1.0 + jnp.exp(-jnp.abs(z)))
        log_stay = log_beta - z
        if mask is not None:
            log_stay = jnp.where(mask, log_stay, 0.0)
        hi = log_stay.astype(bf16)
        lo = (log_stay - hi.astype(f32)).astype(bf16)
        log_betas.append(log_beta)
        sums.append(jnp.dot(jnp.concatenate([hi, lo], axis=1), tri, preferred_element_type=f32))
    totals, contribs = [], {}
    for log_beta, s, (slot, _, _, vblk, mask, after) in zip(log_betas, sums, chains):
        log_w = log_beta + s[:, :KEY_BLOCK]
        c = s[:, KEY_BLOCK:]
        later = totals[after] if after is not None else (None if first else c_ref[slot][...])
        if later is not None:
            log_w = log_w + later
            c = c + later
        totals.append(c)
        w = jnp.exp(log_w)
        if mask is not None:
            w = jnp.where(mask, w, 0.0)
        contrib = jnp.dot(w.astype(bf16), vblk, preferred_element_type=f32)
        contribs[slot] = contrib if slot not in contribs else contribs[slot] + contrib
    oldest = {slot: c for c, (slot, *_) in zip(totals, chains)}
    worst = None
    for slot, c in oldest.items():
        c_ref[slot][...] = c
        worst = c if worst is None else jnp.maximum(worst, c)
    for slot, contrib in contribs.items():
        if first:
            acc_ref[slot][...] = contrib
        else:
            acc_ref[slot][...] += contrib
    return jnp.max(worst)


def _head_lanes(h):
    return slice(h * HEAD_DIM, (h + 1) * HEAD_DIM)


def _attn_prompt_kernel(*refs, n_rider):
    q_ref, kb_ref, vb_ref, tri_ref = refs[:4]
    rider_src, o_ref = refs[4:4 + n_rider], refs[4 + n_rider]
    rider_dst, state = refs[5 + n_rider:5 + 2 * n_rider], refs[5 + 2 * n_rider:]
    acc_ref, c_ref = state[:SB_HEADS], state[SB_HEADS:]
    qb = pl.program_id(1)
    tri = tri_ref[...]
    rows = lax.broadcasted_iota(jnp.int32, (KEY_BLOCK, KEY_BLOCK), 0)
    cols = lax.broadcasted_iota(jnp.int32, (KEY_BLOCK, KEY_BLOCK), 1)
    causal = cols < rows

    def chains(kblock, mask, after_offset=None):
        ks = pl.multiple_of(kblock * KEY_BLOCK, KEY_BLOCK)
        return [(h, q_ref[:, _head_lanes(h)], kb_ref[pl.ds(ks, KEY_BLOCK), _head_lanes(h)],
                 vb_ref[pl.ds(ks, KEY_BLOCK), _head_lanes(h)], mask,
                 None if after_offset is None else after_offset + h) for h in range(SB_HEADS)]

    def write_out():
        for h in range(SB_HEADS):
            o_ref[:, _head_lanes(h)] = acc_ref[h][...].astype(bf16)

    @pl.when(qb == 0)
    def _():
        _run_cast_rider(rider_src, rider_dst)
        _sweep_step(chains(qb, causal), acc_ref, c_ref, tri, True)
        write_out()

    @pl.when(qb > 0)
    def _():
        _run_cast_rider(rider_src, rider_dst)
        worst = _sweep_step(chains(qb, causal) + chains(qb - 1, None, after_offset=0), acc_ref, c_ref, tri, True)

        def more(state):
            kblock, worst = state
            return jnp.logical_and(kblock >= 0, worst > WEIGHT_CUTOFF)

        def step(state):
            kblock, _ = state
            return kblock - 1, _sweep_step(chains(kblock, None), acc_ref, c_ref, tri, False)

        lax.while_loop(more, step, (qb - 2, worst))
        write_out()


def _attn_prompt(act, tri, to_cast, layer, batch, seq):
    act3 = act.reshape(batch, seq, ACT_COLS)
    nq = seq // KEY_BLOCK
    qblock = pl.BlockSpec((None, KEY_BLOCK, SB_WIDTH), lambda b, i: (b, i, ACT_Q))
    whole_k = pl.BlockSpec((None, seq, SB_WIDTH), lambda b, i: (b, 0, ACT_K))
    whole_v = pl.BlockSpec((None, seq, SB_WIDTH), lambda b, i: (b, 0, ACT_V))
    rider_in, rider_out, rider_shapes = _cast_rider(to_cast, layer, batch * nq, lambda b, i: b * nq + i)
    out, *cast = pl.pallas_call(
        functools.partial(_attn_prompt_kernel, n_rider=len(to_cast)),
        grid=(batch, nq),
        in_specs=[qblock, whole_k, whole_v,
                  pl.BlockSpec((2 * KEY_BLOCK, 2 * KEY_BLOCK), lambda b, i: (0, 0))] + rider_in,
        out_specs=[pl.BlockSpec((None, KEY_BLOCK, SB_WIDTH), lambda b, i: (b, i, 0))] + rider_out,
        out_shape=[jax.ShapeDtypeStruct((batch, seq, SB_WIDTH), bf16)] + rider_shapes,
        scratch_shapes=[pltpu.VMEM((KEY_BLOCK, HEAD_DIM), f32)] * (2 * SB_HEADS),
        compiler_params=pltpu.CompilerParams(
            dimension_semantics=("arbitrary", "arbitrary"), vmem_limit_bytes=VMEM_LIMIT),
        name="attn_prompt",
    )(act3, act3, act3, tri, *to_cast)
    return (out.reshape(batch * seq, SB_WIDTH), *cast)


def _attn_sample_kernel(q_ref, kn_ref, vn_ref, tk_ref, tv_ref, ck_hbm, cv_hbm, tri_ref, o_ref,
                        kpad_ref, vpad_ref, kold_ref, vold_ref, sem, *state, layer, n_cache):
    g = pl.program_id(0)
    streams, t_new = q_ref.shape[0], q_ref.shape[1]
    acc_ref, c_ref = state[:streams * SB_HEADS], state[streams * SB_HEADS:]
    n_tail = tk_ref.shape[1] // CACHE_BLOCK_ROWS
    tri = tri_ref[...]
    kpad_ref[...] = jnp.zeros(kpad_ref.shape, bf16)
    vpad_ref[...] = jnp.zeros(vpad_ref.shape, bf16)
    kpad_ref[:, 0:t_new, :] = kn_ref[...]
    vpad_ref[:, 0:t_new, :] = vn_ref[...]
    rows = lax.broadcasted_iota(jnp.int32, (t_new, KEY_BLOCK), 0)
    cols = lax.broadcasted_iota(jnp.int32, (t_new, KEY_BLOCK), 1)

    heads = [(s, h) for s in range(streams) for h in range(SB_HEADS)]
    causal = cols < rows
    new_chains = [(i, q_ref[s, :, _head_lanes(h)], kpad_ref[s, :, _head_lanes(h)], vpad_ref[s, :, _head_lanes(h)],
                   causal, None) for i, (s, h) in enumerate(heads)]
    worst = _sweep_step(new_chains, acc_ref, c_ref, tri, True)

    def cache_chains(k_ref, v_ref, first_row):
        out = []
        for i, (s, h) in enumerate(heads):
            head_rows = pl.ds(first_row + h, KEY_BLOCK, stride=SB_HEADS)
            out.append((i, q_ref[s, :, _head_lanes(h)], k_ref[s, head_rows, :].astype(bf16),
                        v_ref[s, head_rows, :].astype(bf16), None, None))
        return out

    def more(state):
        left, worst = state
        return jnp.logical_and(left > 0, worst > WEIGHT_CUTOFF)

    def tail_step(state):
        left, _ = state
        first_row = pl.multiple_of((left - 1) * CACHE_BLOCK_ROWS, CACHE_BLOCK_ROWS)
        return left - 1, _sweep_step(cache_chains(tk_ref, tv_ref, first_row), acc_ref, c_ref, tri, False)

    _, worst = lax.while_loop(more, tail_step, (n_tail, worst))

    def old_copies(block):
        window = (layer, pl.ds(g * streams, streams), pl.ds(block * CACHE_BLOCK_ROWS, CACHE_BLOCK_ROWS))
        return (pltpu.make_async_copy(ck_hbm.at[window], kold_ref, sem.at[0]),
                pltpu.make_async_copy(cv_hbm.at[window], vold_ref, sem.at[1]))

    def old_step(state):
        left, _ = state
        copies = old_copies(left - 1)
        for cp in copies:
            cp.start()
        for cp in copies:
            cp.wait()
        return left - 1, _sweep_step(cache_chains(kold_ref, vold_ref, 0), acc_ref, c_ref, tri, False)

    lax.while_loop(more, old_step, (n_cache - n_tail, worst))
    for i, (s, h) in enumerate(heads):
        o_ref[s, :, _head_lanes(h)] = acc_ref[i][...].astype(bf16)


def _attn_sample(act, cache_k, cache_v, layer, tri, batch, t_new):
    past = cache_k.shape[2]
    assert t_new <= KEY_BLOCK and past % (KEY_BLOCK * TAIL_BLOCKS) == 0 and batch % SAMPLE_STREAMS == 0
    n_cache = past // KEY_BLOCK
    act3 = act.reshape(batch, t_new, ACT_COLS)
    ck = cache_k.reshape(cache_k.shape[0], batch, past * SB_HEADS, HEAD_DIM)
    cv = cache_v.reshape(cache_v.shape[0], batch, past * SB_HEADS, HEAD_DIM)
    new = pl.BlockSpec((SAMPLE_STREAMS, t_new, SB_WIDTH), lambda g: (g, 0, 0))
    new_q, new_k, new_v = (pl.BlockSpec((SAMPLE_STREAMS, t_new, SB_WIDTH), lambda g, c=c: (g, 0, c))
                           for c in (ACT_Q, ACT_K, ACT_V))
    tail = pl.BlockSpec((None, SAMPLE_STREAMS, TAIL_BLOCKS * CACHE_BLOCK_ROWS, HEAD_DIM),
                        lambda g: (layer, g, n_cache // TAIL_BLOCKS - 1, 0))
    hbm = pl.BlockSpec(memory_space=pl.ANY)
    out = pl.pallas_call(
        functools.partial(_attn_sample_kernel, layer=layer, n_cache=n_cache),
        grid=(batch // SAMPLE_STREAMS,),
        in_specs=[new_q, new_k, new_v, tail, tail, hbm, hbm,
                  pl.BlockSpec((2 * KEY_BLOCK, 2 * KEY_BLOCK), lambda g: (0, 0))],
        out_specs=new,
        out_shape=jax.ShapeDtypeStruct((batch, t_new, SB_WIDTH), bf16),
        scratch_shapes=[
            pltpu.VMEM((SAMPLE_STREAMS, KEY_BLOCK, SB_WIDTH), bf16),
            pltpu.VMEM((SAMPLE_STREAMS, KEY_BLOCK, SB_WIDTH), bf16),
            pltpu.VMEM((SAMPLE_STREAMS, CACHE_BLOCK_ROWS, HEAD_DIM), f32),
            pltpu.VMEM((SAMPLE_STREAMS, CACHE_BLOCK_ROWS, HEAD_DIM), f32),
            pltpu.SemaphoreType.DMA((2,)),
        ] + [pltpu.VMEM((t_new, HEAD_DIM), f32)] * (2 * SAMPLE_STREAMS * SB_HEADS),
        compiler_params=pltpu.CompilerParams(
            dimension_semantics=("arbitrary",), vmem_limit_bytes=VMEM_LIMIT),
        name="attn_sample",
    )(act3, act3, act3, ck, cv, ck, cv, tri)
    return out.reshape(batch * t_new, SB_WIDTH)


def _merge_kernel(a_ref, u_ref, vn_ref, gate_ref, x_ref, ws_ref, bs_ref, wa_ref, wb_ref, wo_ref,
                  h_ref, bb_ref, *, chunk_len):
    tm = a_ref.shape[0]
    rows = lax.broadcasted_iota(jnp.int32, (chunk_len, chunk_len), 0) // CHUNK
    cols = lax.broadcasted_iota(jnp.int32, (chunk_len, chunk_len), 1) // CHUNK
    causal = cols <= rows
    for g in range(SGU_GROUPS):
        w_g = jnp.where(causal, ws_ref[g], 0.0).astype(bf16)
        b_g = bs_ref[g]
        lanes = slice(g * SGU_GROUP_DIM, (g + 1) * SGU_GROUP_DIM)
        for n in range(tm // chunk_len):
            r = slice(n * chunk_len, (n + 1) * chunk_len)
            s = jnp.dot(w_g, vn_ref[r, lanes].astype(bf16), preferred_element_type=f32) + b_g
            bb_ref[r, lanes] = (u_ref[r, lanes].astype(f32) * s).astype(bf16)
    ta = jnp.dot(a_ref[...], wa_ref[...], preferred_element_type=f32)
    tb = jnp.dot(bb_ref[...], wb_ref[...], preferred_element_type=f32)
    merged = (gate_ref[:, :D_MODEL].astype(f32) * ta + gate_ref[:, D_MODEL:].astype(f32) * tb).astype(bf16)
    h_ref[...] = x_ref[...] + jnp.dot(merged, wo_ref[...], preferred_element_type=f32)


def _merge(a, act, vn_all, x, ws, bs, wa, wb, wo, layer, chunk_len, tm):
    m = x.shape[0]
    row = lambda i: (i, 0)
    layer_row = lambda i: (layer * (m // tm) + i, 0)
    const2 = lambda i: (0, 0)
    const3 = lambda i: (0, 0, 0)
    once = pl.Buffered(1)
    assert (ACT_U + 1) * SEG == 2 * D_MODEL
    return pl.pallas_call(
        functools.partial(_merge_kernel, chunk_len=chunk_len),
        grid=(m // tm,),
        in_specs=[
            pl.BlockSpec((tm, SB_WIDTH), row),
            pl.BlockSpec((tm, SGU_WIDTH), lambda i: (i, ACT_U)),
            pl.BlockSpec((tm, SGU_WIDTH), layer_row),
            pl.BlockSpec((tm, 2 * D_MODEL), lambda i: (i, 1)),
            pl.BlockSpec((tm, D_MODEL), row),
            pl.BlockSpec((SGU_GROUPS, chunk_len, chunk_len), const3, pipeline_mode=once),
            pl.BlockSpec((SGU_GROUPS, chunk_len, SGU_GROUP_DIM), const3, pipeline_mode=once),
            pl.BlockSpec((SB_WIDTH, D_MODEL), const2, pipeline_mode=once),
            pl.BlockSpec((SGU_WIDTH, D_MODEL), const2, pipeline_mode=once),
            pl.BlockSpec((D_MODEL, D_MODEL), const2, pipeline_mode=once),
        ],
        out_specs=pl.BlockSpec((tm, D_MODEL), row),
        out_shape=jax.ShapeDtypeStruct((m, D_MODEL), f32),
        scratch_shapes=[pltpu.VMEM((tm, SGU_WIDTH), bf16)],
        compiler_params=pltpu.CompilerParams(
            dimension_semantics=("arbitrary",), vmem_limit_bytes=VMEM_LIMIT),
        name="merge",
    )(a, act, vn_all, act, x, ws, bs, wa, wb, wo)


def _ffn_kernel(*refs, n_rider, final_norm):
    h_ref, g_ref, wg_ref, wu_ref, wd_ref, gf_ref = refs[:6]
    rider_src = refs[6:6 + n_rider]
    o_ref = refs[6 + n_rider]
    rider_dst, hn_ref = refs[7 + n_rider:7 + 2 * n_rider], refs[7 + 2 * n_rider]
    j = pl.program_id(1)
    tm = h_ref.shape[0]

    def partial_sums(first):
        _run_cast_rider(rider_src, rider_dst)
        for r in range(0, tm, ROW_CHUNK):
            rows = slice(r, r + ROW_CHUNK)
            if first:
                hn_ref[rows, :] = _rmsnorm_rows(h_ref[rows, :], g_ref[...]).astype(bf16)
            hn = hn_ref[rows, :]
            gate = jnp.dot(hn, wg_ref[...], preferred_element_type=f32)
            up = jnp.dot(hn, wu_ref[...], preferred_element_type=f32)
            act = (jax.nn.silu(gate) * up).astype(bf16)
            down = jnp.dot(act, wd_ref[...], preferred_element_type=f32)
            o_ref[rows, :] = (h_ref[rows, :] if first else o_ref[rows, :]) + down

    pl.when(j == 0)(functools.partial(partial_sums, True))
    pl.when(j > 0)(functools.partial(partial_sums, False))

    if final_norm:
        @pl.when(j == pl.num_programs(1) - 1)
        def _():
            o_ref[...] = _rmsnorm_rows(o_ref[...], gf_ref[...])


def _ffn(h, g, w_gate_up, w_down, g_final, to_cast, cast_layer, final_norm, tm, tf):
    m = h.shape[0]
    nm, nf = m // tm, D_FF // tf
    row = lambda i, j: (i, 0)
    vec = pl.BlockSpec((1, D_MODEL), lambda i, j: (0, 0))
    rider_in, rider_out, rider_shapes = _cast_rider(to_cast, cast_layer, nm * nf, lambda i, j: i * nf + j)
    return pl.pallas_call(
        functools.partial(_ffn_kernel, n_rider=len(to_cast), final_norm=final_norm),
        grid=(nm, nf),
        in_specs=[
            pl.BlockSpec((tm, D_MODEL), row),
            vec,
            pl.BlockSpec((D_MODEL, tf), lambda i, j: (0, j)),
            pl.BlockSpec((D_MODEL, tf), lambda i, j: (0, j + nf)),
            pl.BlockSpec((tf, D_MODEL), lambda i, j: (j, 0)),
            vec,
        ] + rider_in,
        out_specs=[pl.BlockSpec((tm, D_MODEL), row)] + rider_out,
        out_shape=[jax.ShapeDtypeStruct((m, D_MODEL), f32)] + rider_shapes,
        scratch_shapes=[pltpu.VMEM((tm, D_MODEL), bf16)],
        compiler_params=pltpu.CompilerParams(
            dimension_semantics=("arbitrary", "arbitrary"), vmem_limit_bytes=VMEM_LIMIT),
        name="ffn",
    )(h, g, w_gate_up, w_gate_up, w_down, g_final, *to_cast)


MERGE_WEIGHTS = ("w_branch_a", "w_branch_b", "w_out")
FFN_WEIGHTS = ("w_gate_up", "w_down")


def _trunk(x, cache_k, cache_v, p, half, tri):
    batch, t = x.shape[0], x.shape[1]
    m = batch * t
    depth = p["w_in"].shape[0]
    chunk_len = min(t, SGU_LEN)
    h = x.reshape(m, D_MODEL)
    stacked = None
    if ("w_in", 0) not in half:
        half["w_in", 0] = p["w_in"][0].astype(bf16)
    for l in range(depth):
        missing = [n for n in MERGE_WEIGHTS if (n, l) not in half]
        act, k_all, v_all, vn_all, *cast = _proj(
            h, p["norm_mix"][l][None], half["w_in", l], p["b_gate"][l][None], p["sgu_norm"][l][None],
            stacked, [p[n] for n in missing], l, depth, tm=min(1024, m))
        half.update({(n, l): w for n, w in zip(missing, cast)})
        stacked = (k_all, v_all, vn_all)
        missing = [n for n in FFN_WEIGHTS if (n, l) not in half]
        if cache_k is None:
            a, *cast = _attn_prompt(act, tri, [p[n] for n in missing], l, batch, t)
            half.update({(n, l): w for n, w in zip(missing, cast)})
        else:
            a = _attn_sample(act, cache_k, cache_v, l, tri, batch, t)
            half.update({(n, l): p[n][l].astype(bf16) for n in missing})
        bs = jnp.broadcast_to(p["b_spatial"][l][:, :chunk_len, None], (SGU_GROUPS, chunk_len, SGU_GROUP_DIM))
        h = _merge(a, act, vn_all, h, p["w_spatial"][l][:, :chunk_len, :chunk_len], bs,
                   half["w_branch_a", l], half["w_branch_b", l], half["w_out", l], l, chunk_len, tm=min(256, m))
        next_in = l + 1 < depth and ("w_in", l + 1) not in half
        h, *cast = _ffn(h, p["norm_ffn"][l][None], half["w_gate_up", l], half["w_down", l], p["norm_final"][None],
                        [p["w_in"]] if next_in else [], l + 1, final_norm=(l == depth - 1),
                        tm=min(1024, m), tf=512)
        if next_in:
            half["w_in", l + 1] = cast[0]
    k_all, v_all, vn_all = stacked
    return (h.reshape(batch, t, D_MODEL), k_all.reshape(depth, batch, t, SB_HEADS, HEAD_DIM),
            v_all.reshape(depth, batch, t, SB_HEADS, HEAD_DIM), vn_all.reshape(depth, batch, t, SGU_WIDTH))


def kernel(x_prompt, x_sample, cache_k, cache_v, norm_mix, w_in, b_gate, sgu_norm, w_spatial, b_spatial,
           w_branch_a, w_branch_b, w_out, norm_ffn, w_gate_up, w_down, norm_final):
    p = dict(
        norm_mix=norm_mix, b_gate=b_gate, sgu_norm=sgu_norm, w_spatial=w_spatial, b_spatial=b_spatial,
        norm_ffn=norm_ffn, norm_final=norm_final, w_in=w_in, w_branch_a=w_branch_a, w_branch_b=w_branch_b,
        w_out=w_out, w_gate_up=w_gate_up, w_down=w_down,
    )
    tri = _suffix_matrix()
    half = {}
    y_prompt, k_prompt, v_prompt, _ = _trunk(x_prompt, None, None, p, half, tri)
    y_sample, k_sample, v_sample, sgu_v_sample = _trunk(x_sample, cache_k, cache_v, p, half, tri)
    return (y_prompt, y_sample, k_prompt, v_prompt, k_sample, v_sample, sgu_v_sample)
```

```python
import functools

import jax
import jax.numpy as jnp
from jax import lax
from jax.experimental import pallas as pl
from jax.experimental.pallas import tpu as pltpu

D_MODEL = 2048
HEAD_DIM = 128
SB_HEADS = 8
SB_WIDTH = SB_HEADS * HEAD_DIM
SGU_GROUPS = 8
SGU_GROUP_DIM = 128
SGU_WIDTH = SGU_GROUPS * SGU_GROUP_DIM
SGU_LEN = 128
CHUNK = 64
D_FF = 5632
EPS = 1e-6
KEY_BLOCK = 128
CACHE_BLOCK_ROWS = KEY_BLOCK * SB_HEADS
TAIL_BLOCKS = 2
SAMPLE_STREAMS = 2
Q_SCALE = HEAD_DIM ** -0.5
WEIGHT_CUTOFF = -105.0
SEG = 1024
ROW_CHUNK = 512
BF16_SUBLANES = 16
N_PROJ_TILES = (3 * SB_WIDTH + 2 * SGU_WIDTH + 2 * D_MODEL) // SEG
GATE_TILE0 = (3 * SB_WIDTH + 2 * SGU_WIDTH) // SEG
SGU_V_TILE = GATE_TILE0 - 1
ACT_COLS = (N_PROJ_TILES - 1) * SEG
ACT_Q, ACT_K, ACT_V, ACT_U = 0, 1, 2, 3
PROJ_ROW_CHUNK = 1024
VMEM_LIMIT = 56 * 1024 * 1024

bf16 = jnp.bfloat16
f32 = jnp.float32


def _rmsnorm_rows(x, g):
    return x * lax.rsqrt(jnp.mean(x * x, axis=-1, keepdims=True) + EPS) * g


def _store_head_rows(dst_ref, row0, acc):
    n = acc.shape[0]
    for h in range(SB_HEADS):
        dst_ref[pl.ds(row0 * SB_HEADS + h, n, stride=SB_HEADS), :] = acc[:, h * HEAD_DIM:(h + 1) * HEAD_DIM]


def _slab_rows(rows, n_steps):
    for slab in range(BF16_SUBLANES, rows + 1, BF16_SUBLANES):
        if rows % slab == 0 and rows // slab <= n_steps:
            return slab
    raise ValueError(f"no slab size covers {rows} rows in {n_steps} steps")


def _cast_rider(weights, layer, n_steps, step_of):
    in_specs, out_specs, out_shapes = [], [], []
    for w in weights:
        _, rows, cols = w.shape
        slab = _slab_rows(rows, n_steps)
        last = rows // slab - 1
        in_specs.append(pl.BlockSpec(
            (None, slab, cols), lambda *ids, last=last: (layer, jnp.minimum(step_of(*ids), last), 0)))
        out_specs.append(pl.BlockSpec(
            (slab, cols), lambda *ids, last=last: (jnp.minimum(step_of(*ids), last), 0)))
        out_shapes.append(jax.ShapeDtypeStruct((rows, cols), bf16))
    return in_specs, out_specs, out_shapes


def _run_cast_rider(src_refs, dst_refs):
    for src, dst in zip(src_refs, dst_refs):
        dst[...] = src[...].astype(bf16)


def _proj_kernel(*refs, n_alias, n_rider):
    x_hbm, g_ref, w_ref, bg_ref, sn_ref = refs[:5]
    rider_src = refs[5 + n_alias:5 + n_alias + n_rider]
    outs = refs[5 + n_alias + n_rider:]
    act_ref, k_ref, v_ref, vn_ref = outs[:4]
    rider_dst = outs[4:4 + n_rider]
    xn_ref, x_ref, x_sem = outs[4 + n_rider:]
    i, j = pl.program_id(0), pl.program_id(1)
    tm = x_ref.shape[0]
    chunk = min(tm, PROJ_ROW_CHUNK)

    def x_copy(tile):
        return pltpu.make_async_copy(x_hbm.at[pl.ds(tile * tm, tm), :], x_ref, x_sem.at[0])

    def segment(emit, normalise=False):
        _run_cast_rider(rider_src, rider_dst)
        for r in range(0, tm, chunk):
            rows = slice(r, r + chunk)
            if normalise:
                xn_ref[rows, :] = _rmsnorm_rows(x_ref[rows, :], g_ref[...]).astype(bf16)
            emit(r, rows, jnp.dot(xn_ref[rows, :], w_ref[...], preferred_element_type=f32))

    @pl.when(j == 0)
    def _():
        @pl.when(i == 0)
        def _():
            x_copy(0).start()
        x_copy(i).wait()

        def emit(r, rows, acc):
            act_ref[rows, :] = (acc * Q_SCALE).astype(bf16)
        segment(emit, normalise=True)

    @pl.when(j == 1)
    def _():
        @pl.when(i + 1 < pl.num_programs(0))
        def _():
            x_copy(i + 1).start()

        def emit(r, rows, acc):
            act_ref[rows, :] = acc.astype(bf16)
            _store_head_rows(k_ref, r, acc)
        segment(emit)

    @pl.when(j == 2)
    def _():
        def emit(r, rows, acc):
            act_ref[rows, :] = acc.astype(bf16)
            _store_head_rows(v_ref, r, acc)
        segment(emit)

    @pl.when(j == 3)
    def _():
        def emit(r, rows, acc):
            act_ref[rows, :] = jax.nn.gelu(acc).astype(bf16)
        segment(emit)

    @pl.when(j == SGU_V_TILE)
    def _():
        def emit(r, rows, acc):
            vn_ref[rows, :] = _rmsnorm_rows(jax.nn.gelu(acc), sn_ref[...])
        segment(emit)

    @pl.when(j >= GATE_TILE0)
    def _():
        def emit(r, rows, acc):
            act_ref[rows, :] = jax.nn.sigmoid(acc + bg_ref[...]).astype(bf16)
        segment(emit)


def _proj(x, g, w, bg, sn, stacked, to_cast, layer, depth, tm):
    m = x.shape[0]
    nm = m // tm
    act_col = lambda i, j: (i, j - (j >= SGU_V_TILE).astype(jnp.int32))
    row = lambda i, j: (i, 0)
    layer_row = lambda i, j: (layer * nm + i, 0)
    in_specs = [
        pl.BlockSpec(memory_space=pl.ANY),
        pl.BlockSpec((1, D_MODEL), lambda i, j: (0, 0)),
        pl.BlockSpec((D_MODEL, SEG), lambda i, j: (0, j)),
        pl.BlockSpec((1, SEG), lambda i, j: (0, jnp.maximum(j - GATE_TILE0, 0))),
        pl.BlockSpec((1, SEG), lambda i, j: (0, 0)),
    ]
    args = [x, g, w, bg, sn]
    aliases = {}
    if stacked is not None:
        in_specs += [pl.BlockSpec(memory_space=pl.ANY)] * 3
        args += list(stacked)
        aliases = {5: 1, 6: 2, 7: 3}
    rider_in, rider_out, rider_shapes = _cast_rider(
        to_cast, layer, nm * N_PROJ_TILES, lambda i, j: i * N_PROJ_TILES + j)
    return pl.pallas_call(
        functools.partial(_proj_kernel, n_alias=len(args) - 5, n_rider=len(to_cast)),
        grid=(nm, N_PROJ_TILES),
        in_specs=in_specs + rider_in,
        out_specs=[
            pl.BlockSpec((tm, SEG), act_col),
            pl.BlockSpec((tm * SB_HEADS, HEAD_DIM), layer_row),
            pl.BlockSpec((tm * SB_HEADS, HEAD_DIM), layer_row),
            pl.BlockSpec((tm, SEG), layer_row),
        ] + rider_out,
        out_shape=[
            jax.ShapeDtypeStruct((m, ACT_COLS), bf16),
            jax.ShapeDtypeStruct((depth * m * SB_HEADS, HEAD_DIM), f32),
            jax.ShapeDtypeStruct((depth * m * SB_HEADS, HEAD_DIM), f32),
            jax.ShapeDtypeStruct((depth * m, SGU_WIDTH), f32),
        ] + rider_shapes,
        input_output_aliases=aliases,
        scratch_shapes=[pltpu.VMEM((tm, D_MODEL), bf16), pltpu.VMEM((tm, D_MODEL), f32),
                        pltpu.SemaphoreType.DMA((1,))],
        compiler_params=pltpu.CompilerParams(
            dimension_semantics=("arbitrary", "arbitrary"), vmem_limit_bytes=VMEM_LIMIT),
        name="proj",
    )(*args, *to_cast)


def _suffix_matrix():
    i = jnp.arange(KEY_BLOCK)[:, None]
    jj = jnp.arange(KEY_BLOCK)[None, :]
    strict = (i > jj).astype(bf16)
    half = jnp.concatenate([strict, jnp.ones((KEY_BLOCK, KEY_BLOCK), bf16)], axis=1)
    return jnp.concatenate([half, half], axis=0)


def _sweep_step(chains, acc_ref, c_ref, tri, first):
    zs = [lax.dot_general(q, kblk, (((1,), (1,)), ((), ())), preferred_element_type=f32)
          for _, q, kblk, _, _, _ in chains]
    log_betas, sums = [], []
    for z, (_, _, _, _, mask, _) in zip(zs, chains):
        log_beta = jnp.minimum(z, 0.0) - jnp.log(1.0 + jnp.exp(-jnp.abs(z)))
        log_stay = log_beta - z
        if mask is not None:
            log_stay = jnp.where(mask, log_stay, 0.0)
        hi = log_stay.astype(bf16)
        lo = (log_stay - hi.astype(f32)).astype(bf16)
        log_betas.append(log_beta)
        sums.append(jnp.dot(jnp.concatenate([hi, lo], axis=1), tri, preferred_element_type=f32))
    totals, contribs = [], {}
    for log_beta, s, (slot, _, _, vblk, mask, after) in zip(log_betas, sums, chains):
        log_w = log_beta + s[:, :KEY_BLOCK]
        c = s[:, KEY_BLOCK:]
        later = totals[after] if after is not None else (None if first else c_ref[slot][...])
        if later is not None:
            log_w = log_w + later
            c = c + later
        totals.append(c)
        w = jnp.exp(log_w)
        if mask is not None:
            w = jnp.where(mask, w, 0.0)
        contrib = jnp.dot(w.astype(bf16), vblk, preferred_element_type=f32)
        contribs[slot] = contrib if slot not in contribs else contribs[slot] + contrib
    oldest = {slot: c for c, (slot, *_) in zip(totals, chains)}
    worst = None
    for slot, c in oldest.items():
        c_ref[slot][...] = c
        worst = c if worst is None else jnp.maximum(worst, c)
    for slot, contrib in contribs.items():
        if first:
            acc_ref[slot][...] = contrib
        else:
            acc_ref[slot][...] += contrib
    return jnp.max(worst)


def _head_lanes(h):
    return slice(h * HEAD_DIM, (h + 1) * HEAD_DIM)


def _attn_prompt_kernel(*refs, n_rider):
    q_ref, kb_ref, vb_ref, tri_ref = refs[:4]
    rider_src, o_ref = refs[4:4 + n_rider], refs[4 + n_rider]
    rider_dst, state = refs[5 + n_rider:5 + 2 * n_rider], refs[5 + 2 * n_rider:]
    acc_ref, c_ref = state[:SB_HEADS], state[SB_HEADS:]
    qb = pl.program_id(1)
    tri = tri_ref[...]
    rows = lax.broadcasted_iota(jnp.int32, (KEY_BLOCK, KEY_BLOCK), 0)
    cols = lax.broadcasted_iota(jnp.int32, (KEY_BLOCK, KEY_BLOCK), 1)
    causal = cols < rows

    def chains(kblock, mask, after_offset=None):
        ks = pl.multiple_of(kblock * KEY_BLOCK, KEY_BLOCK)
        return [(h, q_ref[:, _head_lanes(h)], kb_ref[pl.ds(ks, KEY_BLOCK), _head_lanes(h)],
                 vb_ref[pl.ds(ks, KEY_BLOCK), _head_lanes(h)], mask,
                 None if after_offset is None else after_offset + h) for h in range(SB_HEADS)]

    def write_out():
        for h in range(SB_HEADS):
            o_ref[:, _head_lanes(h)] = acc_ref[h][...].astype(bf16)

    @pl.when(qb == 0)
    def _():
        _run_cast_rider(rider_src, rider_dst)
        _sweep_step(chains(qb, causal), acc_ref, c_ref, tri, True)
        write_out()

    @pl.when(qb > 0)
    def _():
        _run_cast_rider(rider_src, rider_dst)
        worst = _sweep_step(chains(qb, causal) + chains(qb - 1, None, after_offset=0), acc_ref, c_ref, tri, True)

        def more(state):
            kblock, worst = state
            return jnp.logical_and(kblock >= 0, worst > WEIGHT_CUTOFF)

        def step(state):
            kblock, _ = state
            return kblock - 1, _sweep_step(chains(kblock, None), acc_ref, c_ref, tri, False)

        lax.while_loop(more, step, (qb - 2, worst))
        write_out()


def _attn_prompt(act, tri, to_cast, layer, batch, seq):
    act3 = act.reshape(batch, seq, ACT_COLS)
    nq = seq // KEY_BLOCK
    qblock = pl.BlockSpec((None, KEY_BLOCK, SB_WIDTH), lambda b, i: (b, i, ACT_Q))
    whole_k = pl.BlockSpec((None, seq, SB_WIDTH), lambda b, i: (b, 0, ACT_K))
    whole_v = pl.BlockSpec((None, seq, SB_WIDTH), lambda b, i: (b, 0, ACT_V))
    rider_in, rider_out, rider_shapes = _cast_rider(to_cast, layer, batch * nq, lambda b, i: b * nq + i)
    out, *cast = pl.pallas_call(
        functools.partial(_attn_prompt_kernel, n_rider=len(to_cast)),
        grid=(batch, nq),
        in_specs=[qblock, whole_k, whole_v,
                  pl.BlockSpec((2 * KEY_BLOCK, 2 * KEY_BLOCK), lambda b, i: (0, 0))] + rider_in,
        out_specs=[pl.BlockSpec((None, KEY_BLOCK, SB_WIDTH), lambda b, i: (b, i, 0))] + rider_out,
        out_shape=[jax.ShapeDtypeStruct((batch, seq, SB_WIDTH), bf16)] + rider_shapes,
        scratch_shapes=[pltpu.VMEM((KEY_BLOCK, HEAD_DIM), f32)] * (2 * SB_HEADS),
        compiler_params=pltpu.CompilerParams(
            dimension_semantics=("arbitrary", "arbitrary"), vmem_limit_bytes=VMEM_LIMIT),
        name="attn_prompt",
    )(act3, act3, act3, tri, *to_cast)
    return (out.reshape(batch * seq, SB_WIDTH), *cast)


def _attn_sample_kernel(q_ref, kn_ref, vn_ref, tk_ref, tv_ref, ck_hbm, cv_hbm, tri_ref, o_ref,
                        kpad_ref, vpad_ref, kold_ref, vold_ref, sem, *state, layer, n_cache):
    g = pl.program_id(0)
    streams, t_new = q_ref.shape[0], q_ref.shape[1]
    acc_ref, c_ref = state[:streams * SB_HEADS], state[streams * SB_HEADS:]
    n_tail = tk_ref.shape[1] // CACHE_BLOCK_ROWS
    tri = tri_ref[...]
    kpad_ref[...] = jnp.zeros(kpad_ref.shape, bf16)
    vpad_ref[...] = jnp.zeros(vpad_ref.shape, bf16)
    kpad_ref[:, 0:t_new, :] = kn_ref[...]
    vpad_ref[:, 0:t_new, :] = vn_ref[...]
    rows = lax.broadcasted_iota(jnp.int32, (t_new, KEY_BLOCK), 0)
    cols = lax.broadcasted_iota(jnp.int32, (t_new, KEY_BLOCK), 1)

    heads = [(s, h) for s in range(streams) for h in range(SB_HEADS)]
    causal = cols < rows
    new_chains = [(i, q_ref[s, :, _head_lanes(h)], kpad_ref[s, :, _head_lanes(h)], vpad_ref[s, :, _head_lanes(h)],
                   causal, None) for i, (s, h) in enumerate(heads)]
    worst = _sweep_step(new_chains, acc_ref, c_ref, tri, True)

    def cache_chains(k_ref, v_ref, first_row):
        out = []
        for i, (s, h) in enumerate(heads):
            head_rows = pl.ds(first_row + h, KEY_BLOCK, stride=SB_HEADS)
            out.append((i, q_ref[s, :, _head_lanes(h)], k_ref[s, head_rows, :].astype(bf16),
                        v_ref[s, head_rows, :].astype(bf16), None, None))
        return out

    def more(state):
        left, worst = state
        return jnp.logical_and(left > 0, worst > WEIGHT_CUTOFF)

    def tail_step(state):
        left, _ = state
        first_row = pl.multiple_of((left - 1) * CACHE_BLOCK_ROWS, CACHE_BLOCK_ROWS)
        return left - 1, _sweep_step(cache_chains(tk_ref, tv_ref, first_row), acc_ref, c_ref, tri, False)

    _, worst = lax.while_loop(more, tail_step, (n_tail, worst))

    def old_copies(block):
        window = (layer, pl.ds(g * streams, streams), pl.ds(block * CACHE_BLOCK_ROWS, CACHE_BLOCK_ROWS))
        return (pltpu.make_async_copy(ck_hbm.at[window], kold_ref, sem.at[0]),
                pltpu.make_async_copy(cv_hbm.at[window], vold_ref, sem.at[1]))

    def old_step(state):
        left, _ = state
        copies = old_copies(left - 1)
        for cp in copies:
            cp.start()
        for cp in copies:
            cp.wait()
        return left - 1, _sweep_step(cache_chains(kold_ref, vold_ref, 0), acc_ref, c_ref, tri, False)

    lax.while_loop(more, old_step, (n_cache - n_tail, worst))
    for i, (s, h) in enumerate(heads):
        o_ref[s, :, _head_lanes(h)] = acc_ref[i][...].astype(bf16)


def _attn_sample(act, cache_k, cache_v, layer, tri, batch, t_new):
    past = cache_k.shape[2]
    assert t_new <= KEY_BLOCK and past % (KEY_BLOCK * TAIL_BLOCKS) == 0 and batch % SAMPLE_STREAMS == 0
    n_cache = past // KEY_BLOCK
    act3 = act.reshape(batch, t_new, ACT_COLS)
    ck = cache_k.reshape(cache_k.shape[0], batch, past * SB_HEADS, HEAD_DIM)
    cv = cache_v.reshape(cache_v.shape[0], batch, past * SB_HEADS, HEAD_DIM)
    new = pl.BlockSpec((SAMPLE_STREAMS, t_new, SB_WIDTH), lambda g: (g, 0, 0))
    new_q, new_k, new_v = (pl.BlockSpec((SAMPLE_STREAMS, t_new, SB_WIDTH), lambda g, c=c: (g, 0, c))
                           for c in (ACT_Q, ACT_K, ACT_V))
    tail = pl.BlockSpec((None, SAMPLE_STREAMS, TAIL_BLOCKS * CACHE_BLOCK_ROWS, HEAD_DIM),
                        lambda g: (layer, g, n_cache // TAIL_BLOCKS - 1, 0))
    hbm = pl.BlockSpec(memory_space=pl.ANY)
    out = pl.pallas_call(
        functools.partial(_attn_sample_kernel, layer=layer, n_cache=n_cache),
        grid=(batch // SAMPLE_STREAMS,),
        in_specs=[new_q, new_k, new_v, tail, tail, hbm, hbm,
                  pl.BlockSpec((2 * KEY_BLOCK, 2 * KEY_BLOCK), lambda g: (0, 0))],
        out_specs=new,
        out_shape=jax.ShapeDtypeStruct((batch, t_new, SB_WIDTH), bf16),
        scratch_shapes=[
            pltpu.VMEM((SAMPLE_STREAMS, KEY_BLOCK, SB_WIDTH), bf16),
            pltpu.VMEM((SAMPLE_STREAMS, KEY_BLOCK, SB_WIDTH), bf16),
            pltpu.VMEM((SAMPLE_STREAMS, CACHE_BLOCK_ROWS, HEAD_DIM), f32),
            pltpu.VMEM((SAMPLE_STREAMS, CACHE_BLOCK_ROWS, HEAD_DIM), f32),
            pltpu.SemaphoreType.DMA((2,)),
        ] + [pltpu.VMEM((t_new, HEAD_DIM), f32)] * (2 * SAMPLE_STREAMS * SB_HEADS),
        compiler_params=pltpu.CompilerParams(
            dimension_semantics=("arbitrary",), vmem_limit_bytes=VMEM_LIMIT),
        name="attn_sample",
    )(act3, act3, act3, ck, cv, ck, cv, tri)
    return out.reshape(batch * t_new, SB_WIDTH)


def _merge_kernel(a_ref, u_ref, vn_ref, gate_ref, x_ref, ws_ref, bs_ref, wa_ref, wb_ref, wo_ref,
                  h_ref, bb_ref, *, chunk_len):
    tm = a_ref.shape[0]
    rows = lax.broadcasted_iota(jnp.int32, (chunk_len, chunk_len), 0) // CHUNK
    cols = lax.broadcasted_iota(jnp.int32, (chunk_len, chunk_len), 1) // CHUNK
    causal = cols <= rows
    for g in range(SGU_GROUPS):
        w_g = jnp.where(causal, ws_ref[g], 0.0).astype(bf16)
        b_g = bs_ref[g]
        lanes = slice(g * SGU_GROUP_DIM, (g + 1) * SGU_GROUP_DIM)
        for n in range(tm // chunk_len):
            r = slice(n * chunk_len, (n + 1) * chunk_len)
            s = jnp.dot(w_g, vn_ref[r, lanes].astype(bf16), preferred_element_type=f32) + b_g
            bb_ref[r, lanes] = (u_ref[r, lanes].astype(f32) * s).astype(bf16)
    ta = jnp.dot(a_ref[...], wa_ref[...], preferred_element_type=f32)
    tb = jnp.dot(bb_ref[...], wb_ref[...], preferred_element_type=f32)
    merged = (gate_ref[:, :D_MODEL].astype(f32) * ta + gate_ref[:, D_MODEL:].astype(f32) * tb).astype(bf16)
    h_ref[...] = x_ref[...] + jnp.dot(merged, wo_ref[...], preferred_element_type=f32)


def _merge(a, act, vn_all, x, ws, bs, wa, wb, wo, layer, chunk_len, tm):
    m = x.shape[0]
    row = lambda i: (i, 0)
    layer_row = lambda i: (layer * (m // tm) + i, 0)
    const2 = lambda i: (0, 0)
    const3 = lambda i: (0, 0, 0)
    once = pl.Buffered(1)
    assert (ACT_U + 1) * SEG == 2 * D_MODEL
    return pl.pallas_call(
        functools.partial(_merge_kernel, chunk_len=chunk_len),
        grid=(m // tm,),
        in_specs=[
            pl.BlockSpec((tm, SB_WIDTH), row),
            pl.BlockSpec((tm, SGU_WIDTH), lambda i: (i, ACT_U)),
            pl.BlockSpec((tm, SGU_WIDTH), layer_row),
            pl.BlockSpec((tm, 2 * D_MODEL), lambda i: (i, 1)),
            pl.BlockSpec((tm, D_MODEL), row),
            pl.BlockSpec((SGU_GROUPS, chunk_len, chunk_len), const3, pipeline_mode=once),
            pl.BlockSpec((SGU_GROUPS, chunk_len, SGU_GROUP_DIM), const3, pipeline_mode=once),
            pl.BlockSpec((SB_WIDTH, D_MODEL), const2, pipeline_mode=once),
            pl.BlockSpec((SGU_WIDTH, D_MODEL), const2, pipeline_mode=once),
            pl.BlockSpec((D_MODEL, D_MODEL), const2, pipeline_mode=once),
        ],
        out_specs=pl.BlockSpec((tm, D_MODEL), row),
        out_shape=jax.ShapeDtypeStruct((m, D_MODEL), f32),
        scratch_shapes=[pltpu.VMEM((tm, SGU_WIDTH), bf16)],
        compiler_params=pltpu.CompilerParams(
            dimension_semantics=("arbitrary",), vmem_limit_bytes=VMEM_LIMIT),
        name="merge",
    )(a, act, vn_all, act, x, ws, bs, wa, wb, wo)


def _ffn_kernel(*refs, n_rider, final_norm):
    h_ref, g_ref, wg_ref, wu_ref, wd_ref, gf_ref = refs[:6]
    rider_src = refs[6:6 + n_rider]
    o_ref = refs[6 + n_rider]
    rider_dst, hn_ref = refs[7 + n_rider:7 + 2 * n_rider], refs[7 + 2 * n_rider]
    j = pl.program_id(1)
    tm = h_ref.shape[0]

    def partial_sums(first):
        _run_cast_rider(rider_src, rider_dst)
        for r in range(0, tm, ROW_CHUNK):
            rows = slice(r, r + ROW_CHUNK)
            if first:
                hn_ref[rows, :] = _rmsnorm_rows(h_ref[rows, :], g_ref[...]).astype(bf16)
            hn = hn_ref[rows, :]
            gate = jnp.dot(hn, wg_ref[...], preferred_element_type=f32)
            up = jnp.dot(hn, wu_ref[...], preferred_element_type=f32)
            act = (jax.nn.silu(gate) * up).astype(bf16)
            down = jnp.dot(act, wd_ref[...], preferred_element_type=f32)
            o_ref[rows, :] = (h_ref[rows, :] if first else o_ref[rows, :]) + down

    pl.when(j == 0)(functools.partial(partial_sums, True))
    pl.when(j > 0)(functools.partial(partial_sums, False))

    if final_norm:
        @pl.when(j == pl.num_programs(1) - 1)
        def _():
            o_ref[...] = _rmsnorm_rows(o_ref[...], gf_ref[...])


def _ffn(h, g, w_gate_up, w_down, g_final, to_cast, cast_layer, final_norm, tm, tf):
    m = h.shape[0]
    nm, nf = m // tm, D_FF // tf
    row = lambda i, j: (i, 0)
    vec = pl.BlockSpec((1, D_MODEL), lambda i, j: (0, 0))
    rider_in, rider_out, rider_shapes = _cast_rider(to_cast, cast_layer, nm * nf, lambda i, j: i * nf + j)
    return pl.pallas_call(
        functools.partial(_ffn_kernel, n_rider=len(to_cast), final_norm=final_norm),
        grid=(nm, nf),
        in_specs=[
            pl.BlockSpec((tm, D_MODEL), row),
            vec,
            pl.BlockSpec((D_MODEL, tf), lambda i, j: (0, j)),
            pl.BlockSpec((D_MODEL, tf), lambda i, j: (0, j + nf)),
            pl.BlockSpec((tf, D_MODEL), lambda i, j: (j, 0)),
            vec,
        ] + rider_in,
        out_specs=[pl.BlockSpec((tm, D_MODEL), row)] + rider_out,
        out_shape=[jax.ShapeDtypeStruct((m, D_MODEL), f32)] + rider_shapes,
        scratch_shapes=[pltpu.VMEM((tm, D_MODEL), bf16)],
        compiler_params=pltpu.CompilerParams(
            dimension_semantics=("arbitrary", "arbitrary"), vmem_limit_bytes=VMEM_LIMIT),
        name="ffn",
    )(h, g, w_gate_up, w_gate_up, w_down, g_final, *to_cast)


MERGE_WEIGHTS = ("w_branch_a", "w_branch_b", "w_out")
FFN_WEIGHTS = ("w_gate_up", "w_down")


def _trunk(x, cache_k, cache_v, p, half, tri):
    batch, t = x.shape[0], x.shape[1]
    m = batch * t
    depth = p["w_in"].shape[0]
    chunk_len = min(t, SGU_LEN)
    h = x.reshape(m, D_MODEL)
    stacked = None
    if ("w_in", 0) not in half:
        half["w_in", 0] = p["w_in"][0].astype(bf16)
    for l in range(depth):
        missing = [n for n in MERGE_WEIGHTS if (n, l) not in half]
        act, k_all, v_all, vn_all, *cast = _proj(
            h, p["norm_mix"][l][None], half["w_in", l], p["b_gate"][l][None], p["sgu_norm"][l][None],
            stacked, [p[n] for n in missing], l, depth, tm=min(1024, m))
        half.update({(n, l): w for n, w in zip(missing, cast)})
        stacked = (k_all, v_all, vn_all)
        missing = [n for n in FFN_WEIGHTS if (n, l) not in half]
        if cache_k is None:
            a, *cast = _attn_prompt(act, tri, [p[n] for n in missing], l, batch, t)
            half.update({(n, l): w for n, w in zip(missing, cast)})
        else:
            a = _attn_sample(act, cache_k, cache_v, l, tri, batch, t)
            half.update({(n, l): p[n][l].astype(bf16) for n in missing})
        bs = jnp.broadcast_to(p["b_spatial"][l][:, :chunk_len, None], (SGU_GROUPS, chunk_len, SGU_GROUP_DIM))
        h = _merge(a, act, vn_all, h, p["w_spatial"][l][:, :chunk_len, :chunk_len], bs,
                   half["w_branch_a", l], half["w_branch_b", l], half["w_out", l], l, chunk_len, tm=min(256, m))
        next_in = l + 1 < depth and ("w_in", l + 1) not in half
        h, *cast = _ffn(h, p["norm_ffn"][l][None], half["w_gate_up", l], half["w_down", l], p["norm_final"][None],
                        [p["w_in"]] if next_in else [], l + 1, final_norm=(l == depth - 1),
                        tm=min(1024, m), tf=512)
        if next_in:
            half["w_in", l + 1] = cast[0]
    k_all, v_all, vn_all = stacked
    return (h.reshape(batch, t, D_MODEL), k_all.reshape(depth, batch, t, SB_HEADS, HEAD_DIM),
            v_all.reshape(depth, batch, t, SB_HEADS, HEAD_DIM), vn_all.reshape(depth, batch, t, SGU_WIDTH))


def kernel(x_prompt, x_sample, cache_k, cache_v, norm_mix, w_in, b_gate, sgu_norm, w_spatial, b_spatial,
           w_branch_a, w_branch_b, w_out, norm_ffn, w_gate_up, w_down, norm_final):
    p = dict(
        norm_mix=norm_mix, b_gate=b_gate, sgu_norm=sgu_norm, w_spatial=w_spatial, b_spatial=b_spatial,
        norm_ffn=norm_ffn, norm_final=norm_final, w_in=w_in, w_branch_a=w_branch_a, w_branch_b=w_branch_b,
        w_out=w_out, w_gate_up=w_gate_up, w_down=w_down,
    )
    tri = _suffix_matrix()
    half = {}
    y_prompt, k_prompt, v_prompt, _ = _trunk(x_prompt, None, None, p, half, tri)
    y_sample, k_sample, v_sample, sgu_v_sample = _trunk(x_sample, cache_k, cache_v, p, half, tri)
    return (y_prompt, y_sample, k_prompt, v_prompt, k_sample, v_sample, sgu_v_sample)
```

```python
import functools

import jax
import jax.numpy as jnp
from jax import lax
from jax.experimental import pallas as pl
from jax.experimental.pallas import tpu as pltpu

D_MODEL = 2048
HEAD_DIM = 128
SB_HEADS = 8
SB_WIDTH = SB_HEADS * HEAD_DIM
SGU_GROUPS = 8
SGU_GROUP_DIM = 128
SGU_WIDTH = SGU_GROUPS * SGU_GROUP_DIM
SGU_LEN = 128
CHUNK = 64
D_FF = 5632
EPS = 1e-6
KEY_BLOCK = 128
CACHE_BLOCK_ROWS = KEY_BLOCK * SB_HEADS
TAIL_BLOCKS = 2
FIRST_STEP_TAIL_BLOCKS = 2
SAMPLE_STREAMS = 2
Q_SCALE = HEAD_DIM ** -0.5
WEIGHT_CUTOFF = -105.0
SEG = 1024
ROW_CHUNK = 512
BF16_SUBLANES = 16
N_PROJ_TILES = (3 * SB_WIDTH + 2 * SGU_WIDTH + 2 * D_MODEL) // SEG
GATE_TILE0 = (3 * SB_WIDTH + 2 * SGU_WIDTH) // SEG
SGU_V_TILE = GATE_TILE0 - 1
ACT_COLS = (N_PROJ_TILES - 1) * SEG
ACT_Q, ACT_K, ACT_V, ACT_U = 0, 1, 2, 3
PROJ_ROW_CHUNK = 1024
VMEM_LIMIT = 56 * 1024 * 1024

bf16 = jnp.bfloat16
f32 = jnp.float32


def _rmsnorm_rows(x, g):
    return x * lax.rsqrt(jnp.mean(x * x, axis=-1, keepdims=True) + EPS) * g


def _store_head_rows(dst_ref, row0, acc):
    n = acc.shape[0]
    for h in range(SB_HEADS):
        dst_ref[pl.ds(row0 * SB_HEADS + h, n, stride=SB_HEADS), :] = acc[:, h * HEAD_DIM:(h + 1) * HEAD_DIM]


def _slab_rows(rows, n_steps):
    for slab in range(BF16_SUBLANES, rows + 1, BF16_SUBLANES):
        if rows % slab == 0 and rows // slab <= n_steps:
            return slab
    raise ValueError(f"no slab size covers {rows} rows in {n_steps} steps")


def _cast_rider(weights, layer, n_steps, step_of):
    in_specs, out_specs, out_shapes = [], [], []
    for w in weights:
        _, rows, cols = w.shape
        slab = _slab_rows(rows, n_steps)
        last = rows // slab - 1
        in_specs.append(pl.BlockSpec(
            (None, slab, cols), lambda *ids, last=last: (layer, jnp.minimum(step_of(*ids), last), 0)))
        out_specs.append(pl.BlockSpec(
            (slab, cols), lambda *ids, last=last: (jnp.minimum(step_of(*ids), last), 0)))
        out_shapes.append(jax.ShapeDtypeStruct((rows, cols), bf16))
    return in_specs, out_specs, out_shapes


def _run_cast_rider(src_refs, dst_refs):
    for src, dst in zip(src_refs, dst_refs):
        dst[...] = src[...].astype(bf16)


def _proj_kernel(*refs, n_alias, n_rider):
    x_hbm, g_ref, w_ref, bg_ref, sn_ref = refs[:5]
    rider_src = refs[5 + n_alias:5 + n_alias + n_rider]
    outs = refs[5 + n_alias + n_rider:]
    act_ref, k_ref, v_ref, vn_ref = outs[:4]
    rider_dst = outs[4:4 + n_rider]
    xn_ref, x_ref, x_sem = outs[4 + n_rider:]
    i, j = pl.program_id(0), pl.program_id(1)
    tm = x_ref.shape[0]
    chunk = min(tm, PROJ_ROW_CHUNK)

    def x_copy(tile):
        return pltpu.make_async_copy(x_hbm.at[pl.ds(tile * tm, tm), :], x_ref, x_sem.at[0])

    def segment(emit, normalise=False):
        _run_cast_rider(rider_src, rider_dst)
        for r in range(0, tm, chunk):
            rows = slice(r, r + chunk)
            if normalise:
                xn_ref[rows, :] = _rmsnorm_rows(x_ref[rows, :], g_ref[...]).astype(bf16)
            emit(r, rows, jnp.dot(xn_ref[rows, :], w_ref[...], preferred_element_type=f32))

    @pl.when(j == 0)
    def _():
        @pl.when(i == 0)
        def _():
            x_copy(0).start()
        x_copy(i).wait()

        def emit(r, rows, acc):
            act_ref[rows, :] = (acc * Q_SCALE).astype(bf16)
        segment(emit, normalise=True)

    @pl.when(j == 1)
    def _():
        @pl.when(i + 1 < pl.num_programs(0))
        def _():
            x_copy(i + 1).start()

        def emit(r, rows, acc):
            act_ref[rows, :] = acc.astype(bf16)
            _store_head_rows(k_ref, r, acc)
        segment(emit)

    @pl.when(j == 2)
    def _():
        def emit(r, rows, acc):
            act_ref[rows, :] = acc.astype(bf16)
            _store_head_rows(v_ref, r, acc)
        segment(emit)

    @pl.when(j == 3)
    def _():
        def emit(r, rows, acc):
            act_ref[rows, :] = jax.nn.gelu(acc).astype(bf16)
        segment(emit)

    @pl.when(j == SGU_V_TILE)
    def _():
        def emit(r, rows, acc):
            vn_ref[rows, :] = _rmsnorm_rows(jax.nn.gelu(acc), sn_ref[...])
        segment(emit)

    @pl.when(j >= GATE_TILE0)
    def _():
        def emit(r, rows, acc):
            act_ref[rows, :] = jax.nn.sigmoid(acc + bg_ref[...]).astype(bf16)
        segment(emit)


def _proj(x, g, w, bg, sn, stacked, to_cast, layer, depth, tm):
    m = x.shape[0]
    nm = m // tm
    act_col = lambda i, j: (i, j - (j >= SGU_V_TILE).astype(jnp.int32))
    row = lambda i, j: (i, 0)
    layer_row = lambda i, j: (layer * nm + i, 0)
    in_specs = [
        pl.BlockSpec(memory_space=pl.ANY),
        pl.BlockSpec((1, D_MODEL), lambda i, j: (0, 0)),
        pl.BlockSpec((D_MODEL, SEG), lambda i, j: (0, j)),
        pl.BlockSpec((1, SEG), lambda i, j: (0, jnp.maximum(j - GATE_TILE0, 0))),
        pl.BlockSpec((1, SEG), lambda i, j: (0, 0)),
    ]
    args = [x, g, w, bg, sn]
    aliases = {}
    if stacked is not None:
        in_specs += [pl.BlockSpec(memory_space=pl.ANY)] * 3
        args += list(stacked)
        aliases = {5: 1, 6: 2, 7: 3}
    rider_in, rider_out, rider_shapes = _cast_rider(
        to_cast, layer, nm * N_PROJ_TILES, lambda i, j: i * N_PROJ_TILES + j)
    return pl.pallas_call(
        functools.partial(_proj_kernel, n_alias=len(args) - 5, n_rider=len(to_cast)),
        grid=(nm, N_PROJ_TILES),
        in_specs=in_specs + rider_in,
        out_specs=[
            pl.BlockSpec((tm, SEG), act_col),
            pl.BlockSpec((tm * SB_HEADS, HEAD_DIM), layer_row),
            pl.BlockSpec((tm * SB_HEADS, HEAD_DIM), layer_row),
            pl.BlockSpec((tm, SEG), layer_row),
        ] + rider_out,
        out_shape=[
            jax.ShapeDtypeStruct((m, ACT_COLS), bf16),
            jax.ShapeDtypeStruct((depth * m * SB_HEADS, HEAD_DIM), f32),
            jax.ShapeDtypeStruct((depth * m * SB_HEADS, HEAD_DIM), f32),
            jax.ShapeDtypeStruct((depth * m, SGU_WIDTH), f32),
        ] + rider_shapes,
        input_output_aliases=aliases,
        scratch_shapes=[pltpu.VMEM((tm, D_MODEL), bf16), pltpu.VMEM((tm, D_MODEL), f32),
                        pltpu.SemaphoreType.DMA((1,))],
        compiler_params=pltpu.CompilerParams(
            dimension_semantics=("arbitrary", "arbitrary"), vmem_limit_bytes=VMEM_LIMIT),
        name="proj",
    )(*args, *to_cast)


def _suffix_matrix():
    i = jnp.arange(KEY_BLOCK)[:, None]
    jj = jnp.arange(KEY_BLOCK)[None, :]
    strict = (i > jj).astype(bf16)
    half = jnp.concatenate([strict, jnp.ones((KEY_BLOCK, KEY_BLOCK), bf16)], axis=1)
    return jnp.concatenate([half, half], axis=0)


def _sweep_step(chains, acc_ref, c_ref, tri, first):
    zs = [lax.dot_general(q, kblk, (((1,), (1,)), ((), ())), preferred_element_type=f32)
          for _, q, kblk, _, _, _ in chains]
    log_betas, sums = [], []
    for z, (_, _, _, _, mask, _) in zip(zs, chains):
        log_beta = jnp.minimum(z, 0.0) - jnp.log(1.0 + jnp.exp(-jnp.abs(z)))
        log_stay = log_beta - z
        if mask is not None:
            log_stay = jnp.where(mask, log_stay, 0.0)
        hi = log_stay.astype(bf16)
        lo = (log_stay - hi.astype(f32)).astype(bf16)
        log_betas.append(log_beta)
        sums.append(jnp.dot(jnp.concatenate([hi, lo], axis=1), tri, preferred_element_type=f32))
    totals, contribs = [], {}
    for log_beta, s, (slot, _, _, vblk, mask, after) in zip(log_betas, sums, chains):
        log_w = log_beta + s[:, :KEY_BLOCK]
        c = s[:, KEY_BLOCK:]
        later = totals[after] if after is not None else (None if first else c_ref[slot][...])
        if later is not None:
            log_w = log_w + later
            c = c + later
        totals.append(c)
        w = jnp.exp(log_w)
        if mask is not None:
            w = jnp.where(mask, w, 0.0)
        contrib = jnp.dot(w.astype(bf16), vblk, preferred_element_type=f32)
        contribs[slot] = contrib if slot not in contribs else contribs[slot] + contrib
    oldest = {slot: c for c, (slot, *_) in zip(totals, chains)}
    worst = None
    for slot, c in oldest.items():
        c_ref[slot][...] = c
        worst = c if worst is None else jnp.maximum(worst, c)
    for slot, contrib in contribs.items():
        if first:
            acc_ref[slot][...] = contrib
        else:
            acc_ref[slot][...] += contrib
    return jnp.max(worst)


def _head_lanes(h):
    return slice(h * HEAD_DIM, (h + 1) * HEAD_DIM)


def _attn_prompt_kernel(*refs, n_rider):
    q_ref, kb_ref, vb_ref, tri_ref = refs[:4]
    rider_src, o_ref = refs[4:4 + n_rider], refs[4 + n_rider]
    rider_dst, state = refs[5 + n_rider:5 + 2 * n_rider], refs[5 + 2 * n_rider:]
    acc_ref, c_ref = state[:SB_HEADS], state[SB_HEADS:]
    qb = pl.program_id(1)
    tri = tri_ref[...]
    rows = lax.broadcasted_iota(jnp.int32, (KEY_BLOCK, KEY_BLOCK), 0)
    cols = lax.broadcasted_iota(jnp.int32, (KEY_BLOCK, KEY_BLOCK), 1)
    causal = cols < rows

    def chains(kblock, mask, after_offset=None):
        ks = pl.multiple_of(kblock * KEY_BLOCK, KEY_BLOCK)
        return [(h, q_ref[:, _head_lanes(h)], kb_ref[pl.ds(ks, KEY_BLOCK), _head_lanes(h)],
                 vb_ref[pl.ds(ks, KEY_BLOCK), _head_lanes(h)], mask,
                 None if after_offset is None else after_offset + h) for h in range(SB_HEADS)]

    def write_out():
        for h in range(SB_HEADS):
            o_ref[:, _head_lanes(h)] = acc_ref[h][...].astype(bf16)

    @pl.when(qb == 0)
    def _():
        _run_cast_rider(rider_src, rider_dst)
        _sweep_step(chains(qb, causal), acc_ref, c_ref, tri, True)
        write_out()

    def sweep_rest(kblock, worst):
        def more(state):
            kblock, worst = state
            return jnp.logical_and(kblock >= 0, worst > WEIGHT_CUTOFF)

        def step(state):
            kblock, _ = state
            return kblock - 1, _sweep_step(chains(kblock, None), acc_ref, c_ref, tri, False)

        lax.while_loop(more, step, (kblock, worst))
        write_out()

    @pl.when(qb == 1)
    def _():
        _run_cast_rider(rider_src, rider_dst)
        _sweep_step(chains(qb, causal) + chains(qb - 1, None, after_offset=0), acc_ref, c_ref, tri, True)
        write_out()

    @pl.when(qb > 1)
    def _():
        _run_cast_rider(rider_src, rider_dst)
        first = (chains(qb, causal) + chains(qb - 1, None, after_offset=0)
                 + chains(qb - 2, None, after_offset=SB_HEADS))
        sweep_rest(qb - 3, _sweep_step(first, acc_ref, c_ref, tri, True))


def _attn_prompt(act, tri, to_cast, layer, batch, seq):
    act3 = act.reshape(batch, seq, ACT_COLS)
    nq = seq // KEY_BLOCK
    qblock = pl.BlockSpec((None, KEY_BLOCK, SB_WIDTH), lambda b, i: (b, i, ACT_Q))
    whole_k = pl.BlockSpec((None, seq, SB_WIDTH), lambda b, i: (b, 0, ACT_K))
    whole_v = pl.BlockSpec((None, seq, SB_WIDTH), lambda b, i: (b, 0, ACT_V))
    rider_in, rider_out, rider_shapes = _cast_rider(to_cast, layer, batch * nq, lambda b, i: b * nq + i)
    out, *cast = pl.pallas_call(
        functools.partial(_attn_prompt_kernel, n_rider=len(to_cast)),
        grid=(batch, nq),
        in_specs=[qblock, whole_k, whole_v,
                  pl.BlockSpec((2 * KEY_BLOCK, 2 * KEY_BLOCK), lambda b, i: (0, 0))] + rider_in,
        out_specs=[pl.BlockSpec((None, KEY_BLOCK, SB_WIDTH), lambda b, i: (b, i, 0))] + rider_out,
        out_shape=[jax.ShapeDtypeStruct((batch, seq, SB_WIDTH), bf16)] + rider_shapes,
        scratch_shapes=[pltpu.VMEM((KEY_BLOCK, HEAD_DIM), f32)] * (2 * SB_HEADS),
        compiler_params=pltpu.CompilerParams(
            dimension_semantics=("arbitrary", "arbitrary"), vmem_limit_bytes=VMEM_LIMIT),
        name="attn_prompt",
    )(act3, act3, act3, tri, *to_cast)
    return (out.reshape(batch * seq, SB_WIDTH), *cast)


def _attn_sample_kernel(q_ref, kn_ref, vn_ref, tk_ref, tv_ref, ck_hbm, cv_hbm, tri_ref, o_ref,
                        kpad_ref, vpad_ref, kold_ref, vold_ref, sem, *state, layer, n_cache):
    g = pl.program_id(0)
    streams, t_new = q_ref.shape[0], q_ref.shape[1]
    acc_ref, c_ref = state[:streams * SB_HEADS], state[streams * SB_HEADS:]
    n_tail = tk_ref.shape[1] // CACHE_BLOCK_ROWS
    tri = tri_ref[...]
    kpad_ref[...] = jnp.zeros(kpad_ref.shape, bf16)
    vpad_ref[...] = jnp.zeros(vpad_ref.shape, bf16)
    kpad_ref[:, 0:t_new, :] = kn_ref[...]
    vpad_ref[:, 0:t_new, :] = vn_ref[...]
    rows = lax.broadcasted_iota(jnp.int32, (t_new, KEY_BLOCK), 0)
    cols = lax.broadcasted_iota(jnp.int32, (t_new, KEY_BLOCK), 1)

    heads = [(s, h) for s in range(streams) for h in range(SB_HEADS)]
    causal = cols < rows
    new_chains = [(i, q_ref[s, :, _head_lanes(h)], kpad_ref[s, :, _head_lanes(h)], vpad_ref[s, :, _head_lanes(h)],
                   causal, None) for i, (s, h) in enumerate(heads)]

    def cache_chains(k_ref, v_ref, first_row, after_offset=None):
        out = []
        for i, (s, h) in enumerate(heads):
            head_rows = pl.ds(first_row + h, KEY_BLOCK, stride=SB_HEADS)
            out.append((i, q_ref[s, :, _head_lanes(h)], k_ref[s, head_rows, :].astype(bf16),
                        v_ref[s, head_rows, :].astype(bf16), None,
                        None if after_offset is None else after_offset + i))
        return out

    first = new_chains
    for t in range(FIRST_STEP_TAIL_BLOCKS):
        first = first + cache_chains(tk_ref, tv_ref, (n_tail - 1 - t) * CACHE_BLOCK_ROWS, after_offset=t * len(heads))
    worst = _sweep_step(first, acc_ref, c_ref, tri, True)

    def more(state):
        left, worst = state
        return jnp.logical_and(left > 0, worst > WEIGHT_CUTOFF)

    def tail_step(state):
        left, _ = state
        first_row = pl.multiple_of((left - 1) * CACHE_BLOCK_ROWS, CACHE_BLOCK_ROWS)
        return left - 1, _sweep_step(cache_chains(tk_ref, tv_ref, first_row), acc_ref, c_ref, tri, False)

    _, worst = lax.while_loop(more, tail_step, (n_tail - FIRST_STEP_TAIL_BLOCKS, worst))

    def old_copies(block):
        window = (layer, pl.ds(g * streams, streams), pl.ds(block * CACHE_BLOCK_ROWS, CACHE_BLOCK_ROWS))
        return (pltpu.make_async_copy(ck_hbm.at[window], kold_ref, sem.at[0]),
                pltpu.make_async_copy(cv_hbm.at[window], vold_ref, sem.at[1]))

    def old_step(state):
        left, _ = state
        copies = old_copies(left - 1)
        for cp in copies:
            cp.start()
        for cp in copies:
            cp.wait()
        return left - 1, _sweep_step(cache_chains(kold_ref, vold_ref, 0), acc_ref, c_ref, tri, False)

    lax.while_loop(more, old_step, (n_cache - n_tail, worst))
    for i, (s, h) in enumerate(heads):
        o_ref[s, :, _head_lanes(h)] = acc_ref[i][...].astype(bf16)


def _attn_sample(act, cache_k, cache_v, layer, tri, batch, t_new):
    past = cache_k.shape[2]
    assert t_new <= KEY_BLOCK and past % (KEY_BLOCK * TAIL_BLOCKS) == 0 and batch % SAMPLE_STREAMS == 0
    n_cache = past // KEY_BLOCK
    act3 = act.reshape(batch, t_new, ACT_COLS)
    ck = cache_k.reshape(cache_k.shape[0], batch, past * SB_HEADS, HEAD_DIM)
    cv = cache_v.reshape(cache_v.shape[0], batch, past * SB_HEADS, HEAD_DIM)
    new = pl.BlockSpec((SAMPLE_STREAMS, t_new, SB_WIDTH), lambda g: (g, 0, 0))
    new_q, new_k, new_v = (pl.BlockSpec((SAMPLE_STREAMS, t_new, SB_WIDTH), lambda g, c=c: (g, 0, c))
                           for c in (ACT_Q, ACT_K, ACT_V))
    tail = pl.BlockSpec((None, SAMPLE_STREAMS, TAIL_BLOCKS * CACHE_BLOCK_ROWS, HEAD_DIM),
                        lambda g: (layer, g, n_cache // TAIL_BLOCKS - 1, 0))
    hbm = pl.BlockSpec(memory_space=pl.ANY)
    out = pl.pallas_call(
        functools.partial(_attn_sample_kernel, layer=layer, n_cache=n_cache),
        grid=(batch // SAMPLE_STREAMS,),
        in_specs=[new_q, new_k, new_v, tail, tail, hbm, hbm,
                  pl.BlockSpec((2 * KEY_BLOCK, 2 * KEY_BLOCK), lambda g: (0, 0))],
        out_specs=new,
        out_shape=jax.ShapeDtypeStruct((batch, t_new, SB_WIDTH), bf16),
        scratch_shapes=[
            pltpu.VMEM((SAMPLE_STREAMS, KEY_BLOCK, SB_WIDTH), bf16),
            pltpu.VMEM((SAMPLE_STREAMS, KEY_BLOCK, SB_WIDTH), bf16),
            pltpu.VMEM((SAMPLE_STREAMS, CACHE_BLOCK_ROWS, HEAD_DIM), f32),
            pltpu.VMEM((SAMPLE_STREAMS, CACHE_BLOCK_ROWS, HEAD_DIM), f32),
            pltpu.SemaphoreType.DMA((2,)),
        ] + [pltpu.VMEM((t_new, HEAD_DIM), f32)] * (2 * SAMPLE_STREAMS * SB_HEADS),
        compiler_params=pltpu.CompilerParams(
            dimension_semantics=("arbitrary",), vmem_limit_bytes=VMEM_LIMIT),
        name="attn_sample",
    )(act3, act3, act3, ck, cv, ck, cv, tri)
    return out.reshape(batch * t_new, SB_WIDTH)


def _merge_kernel(a_ref, u_ref, vn_ref, gate_ref, x_ref, ws_ref, bs_ref, wa_ref, wb_ref, wo_ref,
                  h_ref, bb_ref, *, chunk_len):
    tm = a_ref.shape[0]
    rows = lax.broadcasted_iota(jnp.int32, (chunk_len, chunk_len), 0) // CHUNK
    cols = lax.broadcasted_iota(jnp.int32, (chunk_len, chunk_len), 1) // CHUNK
    causal = cols <= rows
    for g in range(SGU_GROUPS):
        w_g = jnp.where(causal, ws_ref[g], 0.0).astype(bf16)
        b_g = bs_ref[g]
        lanes = slice(g * SGU_GROUP_DIM, (g + 1) * SGU_GROUP_DIM)
        for n in range(tm // chunk_len):
            r = slice(n * chunk_len, (n + 1) * chunk_len)
            s = jnp.dot(w_g, vn_ref[r, lanes].astype(bf16), preferred_element_type=f32) + b_g
            bb_ref[r, lanes] = (u_ref[r, lanes].astype(f32) * s).astype(bf16)
    ta = jnp.dot(a_ref[...], wa_ref[...], preferred_element_type=f32)
    tb = jnp.dot(bb_ref[...], wb_ref[...], preferred_element_type=f32)
    merged = (gate_ref[:, :D_MODEL].astype(f32) * ta + gate_ref[:, D_MODEL:].astype(f32) * tb).astype(bf16)
    h_ref[...] = x_ref[...] + jnp.dot(merged, wo_ref[...], preferred_element_type=f32)


def _merge(a, act, vn_all, x, ws, bs, wa, wb, wo, layer, chunk_len, tm):
    m = x.shape[0]
    row = lambda i: (i, 0)
    layer_row = lambda i: (layer * (m // tm) + i, 0)
    const2 = lambda i: (0, 0)
    const3 = lambda i: (0, 0, 0)
    once = pl.Buffered(1)
    assert (ACT_U + 1) * SEG == 2 * D_MODEL
    return pl.pallas_call(
        functools.partial(_merge_kernel, chunk_len=chunk_len),
        grid=(m // tm,),
        in_specs=[
            pl.BlockSpec((tm, SB_WIDTH), row),
            pl.BlockSpec((tm, SGU_WIDTH), lambda i: (i, ACT_U)),
            pl.BlockSpec((tm, SGU_WIDTH), layer_row),
            pl.BlockSpec((tm, 2 * D_MODEL), lambda i: (i, 1)),
            pl.BlockSpec((tm, D_MODEL), row),
            pl.BlockSpec((SGU_GROUPS, chunk_len, chunk_len), const3, pipeline_mode=once),
            pl.BlockSpec((SGU_GROUPS, chunk_len, SGU_GROUP_DIM), const3, pipeline_mode=once),
            pl.BlockSpec((SB_WIDTH, D_MODEL), const2, pipeline_mode=once),
            pl.BlockSpec((SGU_WIDTH, D_MODEL), const2, pipeline_mode=once),
            pl.BlockSpec((D_MODEL, D_MODEL), const2, pipeline_mode=once),
        ],
        out_specs=pl.BlockSpec((tm, D_MODEL), row),
        out_shape=jax.ShapeDtypeStruct((m, D_MODEL), f32),
        scratch_shapes=[pltpu.VMEM((tm, SGU_WIDTH), bf16)],
        compiler_params=pltpu.CompilerParams(
            dimension_semantics=("arbitrary",), vmem_limit_bytes=VMEM_LIMIT),
        name="merge",
    )(a, act, vn_all, act, x, ws, bs, wa, wb, wo)


def _ffn_kernel(*refs, n_rider, final_norm):
    h_ref, g_ref, wg_ref, wu_ref, wd_ref, gf_ref = refs[:6]
    rider_src = refs[6:6 + n_rider]
    o_ref = refs[6 + n_rider]
    rider_dst, hn_ref = refs[7 + n_rider:7 + 2 * n_rider], refs[7 + 2 * n_rider]
    j = pl.program_id(1)
    tm = h_ref.shape[0]

    def partial_sums(first):
        _run_cast_rider(rider_src, rider_dst)
        for r in range(0, tm, ROW_CHUNK):
            rows = slice(r, r + ROW_CHUNK)
            if first:
                hn_ref[rows, :] = _rmsnorm_rows(h_ref[rows, :], g_ref[...]).astype(bf16)
            hn = hn_ref[rows, :]
            gate = jnp.dot(hn, wg_ref[...], preferred_element_type=f32)
            up = jnp.dot(hn, wu_ref[...], preferred_element_type=f32)
            act = (jax.nn.silu(gate) * up).astype(bf16)
            down = jnp.dot(act, wd_ref[...], preferred_element_type=f32)
            o_ref[rows, :] = (h_ref[rows, :] if first else o_ref[rows, :]) + down

    pl.when(j == 0)(functools.partial(partial_sums, True))
    pl.when(j > 0)(functools.partial(partial_sums, False))

    if final_norm:
        @pl.when(j == pl.num_programs(1) - 1)
        def _():
            o_ref[...] = _rmsnorm_rows(o_ref[...], gf_ref[...])


def _ffn(h, g, w_gate_up, w_down, g_final, to_cast, cast_layer, final_norm, tm, tf):
    m = h.shape[0]
    nm, nf = m // tm, D_FF // tf
    row = lambda i, j: (i, 0)
    vec = pl.BlockSpec((1, D_MODEL), lambda i, j: (0, 0))
    rider_in, rider_out, rider_shapes = _cast_rider(to_cast, cast_layer, nm * nf, lambda i, j: i * nf + j)
    return pl.pallas_call(
        functools.partial(_ffn_kernel, n_rider=len(to_cast), final_norm=final_norm),
        grid=(nm, nf),
        in_specs=[
            pl.BlockSpec((tm, D_MODEL), row),
            vec,
            pl.BlockSpec((D_MODEL, tf), lambda i, j: (0, j)),
            pl.BlockSpec((D_MODEL, tf), lambda i, j: (0, j + nf)),
            pl.BlockSpec((tf, D_MODEL), lambda i, j: (j, 0)),
            vec,
        ] + rider_in,
        out_specs=[pl.BlockSpec((tm, D_MODEL), row)] + rider_out,
        out_shape=[jax.ShapeDtypeStruct((m, D_MODEL), f32)] + rider_shapes,
        scratch_shapes=[pltpu.VMEM((tm, D_MODEL), bf16)],
        compiler_params=pltpu.CompilerParams(
            dimension_semantics=("arbitrary", "arbitrary"), vmem_limit_bytes=VMEM_LIMIT),
        name="ffn",
    )(h, g, w_gate_up, w_gate_up, w_down, g_final, *to_cast)


MERGE_WEIGHTS = ("w_branch_a", "w_branch_b", "w_out")
FFN_WEIGHTS = ("w_gate_up", "w_down")


def _trunk(x, cache_k, cache_v, p, half, tri):
    batch, t = x.shape[0], x.shape[1]
    m = batch * t
    depth = p["w_in"].shape[0]
    chunk_len = min(t, SGU_LEN)
    h = x.reshape(m, D_MODEL)
    stacked = None
    if ("w_in", 0) not in half:
        half["w_in", 0] = p["w_in"][0].astype(bf16)
    for l in range(depth):
        missing = [n for n in MERGE_WEIGHTS if (n, l) not in half]
        act, k_all, v_all, vn_all, *cast = _proj(
            h, p["norm_mix"][l][None], half["w_in", l], p["b_gate"][l][None], p["sgu_norm"][l][None],
            stacked, [p[n] for n in missing], l, depth, tm=min(1024, m))
        half.update({(n, l): w for n, w in zip(missing, cast)})
        stacked = (k_all, v_all, vn_all)
        missing = [n for n in FFN_WEIGHTS if (n, l) not in half]
        if cache_k is None:
            a, *cast = _attn_prompt(act, tri, [p[n] for n in missing], l, batch, t)
            half.update({(n, l): w for n, w in zip(missing, cast)})
        else:
            a = _attn_sample(act, cache_k, cache_v, l, tri, batch, t)
            half.update({(n, l): p[n][l].astype(bf16) for n in missing})
        bs = jnp.broadcast_to(p["b_spatial"][l][:, :chunk_len, None], (SGU_GROUPS, chunk_len, SGU_GROUP_DIM))
        h = _merge(a, act, vn_all, h, p["w_spatial"][l][:, :chunk_len, :chunk_len], bs,
                   half["w_branch_a", l], half["w_branch_b", l], half["w_out", l], l, chunk_len, tm=min(256, m))
        next_in = l + 1 < depth and ("w_in", l + 1) not in half
        h, *cast = _ffn(h, p["norm_ffn"][l][None], half["w_gate_up", l], half["w_down", l], p["norm_final"][None],
                        [p["w_in"]] if next_in else [], l + 1, final_norm=(l == depth - 1),
                        tm=min(1024, m), tf=512)
        if next_in:
            half["w_in", l + 1] = cast[0]
    k_all, v_all, vn_all = stacked
    return (h.reshape(batch, t, D_MODEL), k_all.reshape(depth, batch, t, SB_HEADS, HEAD_DIM),
            v_all.reshape(depth, batch, t, SB_HEADS, HEAD_DIM), vn_all.reshape(depth, batch, t, SGU_WIDTH))


def kernel(x_prompt, x_sample, cache_k, cache_v, norm_mix, w_in, b_gate, sgu_norm, w_spatial, b_spatial,
           w_branch_a, w_branch_b, w_out, norm_ffn, w_gate_up, w_down, norm_final):
    p = dict(
        norm_mix=norm_mix, b_gate=b_gate, sgu_norm=sgu_norm, w_spatial=w_spatial, b_spatial=b_spatial,
        norm_ffn=norm_ffn, norm_final=norm_final, w_in=w_in, w_branch_a=w_branch_a, w_branch_b=w_branch_b,
        w_out=w_out, w_gate_up=w_gate_up, w_down=w_down,
    )
    tri = _suffix_matrix()
    half = {}
    y_prompt, k_prompt, v_prompt, _ = _trunk(x_prompt, None, None, p, half, tri)
    y_sample, k_sample, v_sample, sgu_v_sample = _trunk(x_sample, cache_k, cache_v, p, half, tri)
    return (y_prompt, y_sample, k_prompt, v_prompt, k_sample, v_sample, sgu_v_sample)
```

```python
import functools

import jax
import jax.numpy as jnp
from jax import lax
from jax.experimental import pallas as pl
from jax.experimental.pallas import tpu as pltpu

D_MODEL = 2048
HEAD_DIM = 128
SB_HEADS = 8
SB_WIDTH = SB_HEADS * HEAD_DIM
SGU_GROUPS = 8
SGU_GROUP_DIM = 128
SGU_WIDTH = SGU_GROUPS * SGU_GROUP_DIM
SGU_LEN = 128
CHUNK = 64
D_FF = 5632
EPS = 1e-6
KEY_BLOCK = 128
CACHE_BLOCK_ROWS = KEY_BLOCK * SB_HEADS
TAIL_BLOCKS = 2
FIRST_STEP_TAIL_BLOCKS = 2
SAMPLE_STREAMS = 2
Q_SCALE = HEAD_DIM ** -0.5
WEIGHT_CUTOFF = -105.0
SEG = 1024
PROJ_ROWS = 1024
MERGE_ROWS = 256
FFN_ROWS = 1024
FFN_COLS = 512
FFN_ROW_CHUNK = 512
BF16_SUBLANES = 16
N_PROJ_TILES = (3 * SB_WIDTH + 2 * SGU_WIDTH + 2 * D_MODEL) // SEG
GATE_TILE0 = (3 * SB_WIDTH + 2 * SGU_WIDTH) // SEG
SGU_V_TILE = GATE_TILE0 - 1
ACT_COLS = (N_PROJ_TILES - 1) * SEG
ACT_Q, ACT_K, ACT_V, ACT_U = 0, 1, 2, 3
PROJ_ROW_CHUNK = PROJ_ROWS
VMEM_LIMIT = 56 * 1024 * 1024

bf16 = jnp.bfloat16
f32 = jnp.float32


def _rmsnorm_rows(x, g):
    return x * lax.rsqrt(jnp.mean(x * x, axis=-1, keepdims=True) + EPS) * g


def _store_head_rows(dst_ref, row0, acc):
    n = acc.shape[0]
    for h in range(SB_HEADS):
        dst_ref[pl.ds(row0 * SB_HEADS + h, n, stride=SB_HEADS), :] = acc[:, h * HEAD_DIM:(h + 1) * HEAD_DIM]


def _slab_rows(rows, n_steps):
    for slab in range(BF16_SUBLANES, rows + 1, BF16_SUBLANES):
        if rows % slab == 0 and rows // slab <= n_steps:
            return slab
    raise ValueError(f"no slab size covers {rows} rows in {n_steps} steps")


def _cast_rider(weights, layer, n_steps, step_of):
    in_specs, out_specs, out_shapes = [], [], []
    for w in weights:
        _, rows, cols = w.shape
        slab = _slab_rows(rows, n_steps)
        last = rows // slab - 1
        in_specs.append(pl.BlockSpec(
            (None, slab, cols), lambda *ids, last=last: (layer, jnp.minimum(step_of(*ids), last), 0)))
        out_specs.append(pl.BlockSpec(
            (slab, cols), lambda *ids, last=last: (jnp.minimum(step_of(*ids), last), 0)))
        out_shapes.append(jax.ShapeDtypeStruct((rows, cols), bf16))
    return in_specs, out_specs, out_shapes


def _run_cast_rider(src_refs, dst_refs):
    for src, dst in zip(src_refs, dst_refs):
        dst[...] = src[...].astype(bf16)


def _proj_kernel(*refs, n_alias, n_rider):
    x_hbm, g_ref, w_ref, bg_ref, sn_ref = refs[:5]
    rider_src = refs[5 + n_alias:5 + n_alias + n_rider]
    outs = refs[5 + n_alias + n_rider:]
    act_ref, k_ref, v_ref, vn_ref = outs[:4]
    rider_dst = outs[4:4 + n_rider]
    xn_ref, x_ref, x_sem = outs[4 + n_rider:]
    i, j = pl.program_id(0), pl.program_id(1)
    tm = x_ref.shape[0]
    chunk = min(tm, PROJ_ROW_CHUNK)

    def x_copy(tile):
        return pltpu.make_async_copy(x_hbm.at[pl.ds(tile * tm, tm), :], x_ref, x_sem.at[0])

    def segment(emit, normalise=False):
        _run_cast_rider(rider_src, rider_dst)
        for r in range(0, tm, chunk):
            rows = slice(r, r + chunk)
            if normalise:
                xn_ref[rows, :] = _rmsnorm_rows(x_ref[rows, :], g_ref[...]).astype(bf16)
            emit(r, rows, jnp.dot(xn_ref[rows, :], w_ref[...], preferred_element_type=f32))

    @pl.when(j == 0)
    def _():
        @pl.when(i == 0)
        def _():
            x_copy(0).start()
        x_copy(i).wait()

        def emit(r, rows, acc):
            act_ref[rows, :] = (acc * Q_SCALE).astype(bf16)
        segment(emit, normalise=True)

    @pl.when(j == 1)
    def _():
        @pl.when(i + 1 < pl.num_programs(0))
        def _():
            x_copy(i + 1).start()

        def emit(r, rows, acc):
            act_ref[rows, :] = acc.astype(bf16)
            _store_head_rows(k_ref, r, acc)
        segment(emit)

    @pl.when(j == 2)
    def _():
        def emit(r, rows, acc):
            act_ref[rows, :] = acc.astype(bf16)
            _store_head_rows(v_ref, r, acc)
        segment(emit)

    @pl.when(j == 3)
    def _():
        def emit(r, rows, acc):
            act_ref[rows, :] = jax.nn.gelu(acc).astype(bf16)
        segment(emit)

    @pl.when(j == SGU_V_TILE)
    def _():
        def emit(r, rows, acc):
            vn_ref[rows, :] = _rmsnorm_rows(jax.nn.gelu(acc), sn_ref[...])
        segment(emit)

    @pl.when(j >= GATE_TILE0)
    def _():
        def emit(r, rows, acc):
            act_ref[rows, :] = jax.nn.sigmoid(acc + bg_ref[...]).astype(bf16)
        segment(emit)


def _proj(x, g, w, bg, sn, stacked, to_cast, layer, depth, tm):
    m = x.shape[0]
    nm = m // tm
    act_col = lambda i, j: (i, j - (j >= SGU_V_TILE).astype(jnp.int32))
    row = lambda i, j: (i, 0)
    layer_row = lambda i, j: (layer * nm + i, 0)
    in_specs = [
        pl.BlockSpec(memory_space=pl.ANY),
        pl.BlockSpec((1, D_MODEL), lambda i, j: (0, 0)),
        pl.BlockSpec((D_MODEL, SEG), lambda i, j: (0, j)),
        pl.BlockSpec((1, SEG), lambda i, j: (0, jnp.maximum(j - GATE_TILE0, 0))),
        pl.BlockSpec((1, SEG), lambda i, j: (0, 0)),
    ]
    args = [x, g, w, bg, sn]
    aliases = {}
    if stacked is not None:
        in_specs += [pl.BlockSpec(memory_space=pl.ANY)] * 3
        args += list(stacked)
        aliases = {5: 1, 6: 2, 7: 3}
    rider_in, rider_out, rider_shapes = _cast_rider(
        to_cast, layer, nm * N_PROJ_TILES, lambda i, j: i * N_PROJ_TILES + j)
    return pl.pallas_call(
        functools.partial(_proj_kernel, n_alias=len(args) - 5, n_rider=len(to_cast)),
        grid=(nm, N_PROJ_TILES),
        in_specs=in_specs + rider_in,
        out_specs=[
            pl.BlockSpec((tm, SEG), act_col),
            pl.BlockSpec((tm * SB_HEADS, HEAD_DIM), layer_row),
            pl.BlockSpec((tm * SB_HEADS, HEAD_DIM), layer_row),
            pl.BlockSpec((tm, SEG), layer_row),
        ] + rider_out,
        out_shape=[
            jax.ShapeDtypeStruct((m, ACT_COLS), bf16),
            jax.ShapeDtypeStruct((depth * m * SB_HEADS, HEAD_DIM), f32),
            jax.ShapeDtypeStruct((depth * m * SB_HEADS, HEAD_DIM), f32),
            jax.ShapeDtypeStruct((depth * m, SGU_WIDTH), f32),
        ] + rider_shapes,
        input_output_aliases=aliases,
        scratch_shapes=[pltpu.VMEM((tm, D_MODEL), bf16), pltpu.VMEM((tm, D_MODEL), f32),
                        pltpu.SemaphoreType.DMA((1,))],
        compiler_params=pltpu.CompilerParams(
            dimension_semantics=("arbitrary", "arbitrary"), vmem_limit_bytes=VMEM_LIMIT),
        name="proj",
    )(*args, *to_cast)


def _suffix_matrix():
    i = jnp.arange(KEY_BLOCK)[:, None]
    jj = jnp.arange(KEY_BLOCK)[None, :]
    strict = (i > jj).astype(bf16)
    half = jnp.concatenate([strict, jnp.ones((KEY_BLOCK, KEY_BLOCK), bf16)], axis=1)
    return jnp.concatenate([half, half], axis=0)


def _sweep_step(chains, acc_ref, c_ref, tri, first):
    zs = [lax.dot_general(q, kblk, (((1,), (1,)), ((), ())), preferred_element_type=f32)
          for _, q, kblk, _, _, _ in chains]
    log_betas, sums = [], []
    for z, (_, _, _, _, mask, _) in zip(zs, chains):
        log_beta = jnp.minimum(z, 0.0) - jnp.log(1.0 + jnp.exp(-jnp.abs(z)))
        log_stay = log_beta - z
        if mask is not None:
            log_stay = jnp.where(mask, log_stay, 0.0)
        hi = log_stay.astype(bf16)
        lo = (log_stay - hi.astype(f32)).astype(bf16)
        log_betas.append(log_beta)
        sums.append(jnp.dot(jnp.concatenate([hi, lo], axis=1), tri, preferred_element_type=f32))
    totals, contribs = [], {}
    for log_beta, s, (slot, _, _, vblk, mask, after) in zip(log_betas, sums, chains):
        log_w = log_beta + s[:, :KEY_BLOCK]
        c = s[:, KEY_BLOCK:]
        later = totals[after] if after is not None else (None if first else c_ref[slot][...])
        if later is not None:
            log_w = log_w + later
            c = c + later
        totals.append(c)
        w = jnp.exp(log_w)
        if mask is not None:
            w = jnp.where(mask, w, 0.0)
        contrib = jnp.dot(w.astype(bf16), vblk, preferred_element_type=f32)
        contribs[slot] = contrib if slot not in contribs else contribs[slot] + contrib
    oldest = {slot: c for c, (slot, *_) in zip(totals, chains)}
    worst = None
    for slot, c in oldest.items():
        c_ref[slot][...] = c
        worst = c if worst is None else jnp.maximum(worst, c)
    for slot, contrib in contribs.items():
        if first:
            acc_ref[slot][...] = contrib
        else:
            acc_ref[slot][...] += contrib
    return jnp.max(worst)


def _head_lanes(h):
    return slice(h * HEAD_DIM, (h + 1) * HEAD_DIM)


def _attn_prompt_kernel(*refs, n_rider):
    q_ref, kb_ref, vb_ref, tri_ref = refs[:4]
    rider_src, o_ref = refs[4:4 + n_rider], refs[4 + n_rider]
    rider_dst, state = refs[5 + n_rider:5 + 2 * n_rider], refs[5 + 2 * n_rider:]
    acc_ref, c_ref = state[:SB_HEADS], state[SB_HEADS:]
    qb = pl.program_id(1)
    tri = tri_ref[...]
    rows = lax.broadcasted_iota(jnp.int32, (KEY_BLOCK, KEY_BLOCK), 0)
    cols = lax.broadcasted_iota(jnp.int32, (KEY_BLOCK, KEY_BLOCK), 1)
    causal = cols < rows

    def chains(kblock, mask, after_offset=None):
        ks = pl.multiple_of(kblock * KEY_BLOCK, KEY_BLOCK)
        return [(h, q_ref[:, _head_lanes(h)], kb_ref[pl.ds(ks, KEY_BLOCK), _head_lanes(h)],
                 vb_ref[pl.ds(ks, KEY_BLOCK), _head_lanes(h)], mask,
                 None if after_offset is None else after_offset + h) for h in range(SB_HEADS)]

    def write_out():
        for h in range(SB_HEADS):
            o_ref[:, _head_lanes(h)] = acc_ref[h][...].astype(bf16)

    @pl.when(qb == 0)
    def _():
        _run_cast_rider(rider_src, rider_dst)
        _sweep_step(chains(qb, causal), acc_ref, c_ref, tri, True)
        write_out()

    def sweep_rest(kblock, worst):
        def more(state):
            kblock, worst = state
            return jnp.logical_and(kblock >= 0, worst > WEIGHT_CUTOFF)

        def step(state):
            kblock, _ = state
            return kblock - 1, _sweep_step(chains(kblock, None), acc_ref, c_ref, tri, False)

        lax.while_loop(more, step, (kblock, worst))
        write_out()

    @pl.when(qb == 1)
    def _():
        _run_cast_rider(rider_src, rider_dst)
        _sweep_step(chains(qb, causal) + chains(qb - 1, None, after_offset=0), acc_ref, c_ref, tri, True)
        write_out()

    @pl.when(qb > 1)
    def _():
        _run_cast_rider(rider_src, rider_dst)
        first = (chains(qb, causal) + chains(qb - 1, None, after_offset=0)
                 + chains(qb - 2, None, after_offset=SB_HEADS))
        sweep_rest(qb - 3, _sweep_step(first, acc_ref, c_ref, tri, True))


def _attn_prompt(act, tri, to_cast, layer, batch, seq):
    act3 = act.reshape(batch, seq, ACT_COLS)
    nq = seq // KEY_BLOCK
    qblock = pl.BlockSpec((None, KEY_BLOCK, SB_WIDTH), lambda b, i: (b, i, ACT_Q))
    whole_k = pl.BlockSpec((None, seq, SB_WIDTH), lambda b, i: (b, 0, ACT_K))
    whole_v = pl.BlockSpec((None, seq, SB_WIDTH), lambda b, i: (b, 0, ACT_V))
    rider_in, rider_out, rider_shapes = _cast_rider(to_cast, layer, batch * nq, lambda b, i: b * nq + i)
    out, *cast = pl.pallas_call(
        functools.partial(_attn_prompt_kernel, n_rider=len(to_cast)),
        grid=(batch, nq),
        in_specs=[qblock, whole_k, whole_v,
                  pl.BlockSpec((2 * KEY_BLOCK, 2 * KEY_BLOCK), lambda b, i: (0, 0))] + rider_in,
        out_specs=[pl.BlockSpec((None, KEY_BLOCK, SB_WIDTH), lambda b, i: (b, i, 0))] + rider_out,
        out_shape=[jax.ShapeDtypeStruct((batch, seq, SB_WIDTH), bf16)] + rider_shapes,
        scratch_shapes=[pltpu.VMEM((KEY_BLOCK, HEAD_DIM), f32)] * (2 * SB_HEADS),
        compiler_params=pltpu.CompilerParams(
            dimension_semantics=("arbitrary", "arbitrary"), vmem_limit_bytes=VMEM_LIMIT),
        name="attn_prompt",
    )(act3, act3, act3, tri, *to_cast)
    return (out.reshape(batch * seq, SB_WIDTH), *cast)


def _attn_sample_kernel(q_ref, kn_ref, vn_ref, tk_ref, tv_ref, ck_hbm, cv_hbm, tri_ref, o_ref,
                        kpad_ref, vpad_ref, kold_ref, vold_ref, sem, *state, layer, n_cache):
    g = pl.program_id(0)
    streams, t_new = q_ref.shape[0], q_ref.shape[1]
    acc_ref, c_ref = state[:streams * SB_HEADS], state[streams * SB_HEADS:]
    n_tail = tk_ref.shape[1] // CACHE_BLOCK_ROWS
    tri = tri_ref[...]
    kpad_ref[...] = jnp.zeros(kpad_ref.shape, bf16)
    vpad_ref[...] = jnp.zeros(vpad_ref.shape, bf16)
    kpad_ref[:, 0:t_new, :] = kn_ref[...]
    vpad_ref[:, 0:t_new, :] = vn_ref[...]
    rows = lax.broadcasted_iota(jnp.int32, (t_new, KEY_BLOCK), 0)
    cols = lax.broadcasted_iota(jnp.int32, (t_new, KEY_BLOCK), 1)

    heads = [(s, h) for s in range(streams) for h in range(SB_HEADS)]
    causal = cols < rows
    new_chains = [(i, q_ref[s, :, _head_lanes(h)], kpad_ref[s, :, _head_lanes(h)], vpad_ref[s, :, _head_lanes(h)],
                   causal, None) for i, (s, h) in enumerate(heads)]

    def cache_chains(k_ref, v_ref, first_row, after_offset=None):
        out = []
        for i, (s, h) in enumerate(heads):
            head_rows = pl.ds(first_row + h, KEY_BLOCK, stride=SB_HEADS)
            out.append((i, q_ref[s, :, _head_lanes(h)], k_ref[s, head_rows, :].astype(bf16),
                        v_ref[s, head_rows, :].astype(bf16), None,
                        None if after_offset is None else after_offset + i))
        return out

    first = new_chains
    for t in range(FIRST_STEP_TAIL_BLOCKS):
        first = first + cache_chains(tk_ref, tv_ref, (n_tail - 1 - t) * CACHE_BLOCK_ROWS, after_offset=t * len(heads))
    worst = _sweep_step(first, acc_ref, c_ref, tri, True)

    def more(state):
        left, worst = state
        return jnp.logical_and(left > 0, worst > WEIGHT_CUTOFF)

    def tail_step(state):
        left, _ = state
        first_row = pl.multiple_of((left - 1) * CACHE_BLOCK_ROWS, CACHE_BLOCK_ROWS)
        return left - 1, _sweep_step(cache_chains(tk_ref, tv_ref, first_row), acc_ref, c_ref, tri, False)

    _, worst = lax.while_loop(more, tail_step, (n_tail - FIRST_STEP_TAIL_BLOCKS, worst))

    def old_copies(block):
        window = (layer, pl.ds(g * streams, streams), pl.ds(block * CACHE_BLOCK_ROWS, CACHE_BLOCK_ROWS))
        return (pltpu.make_async_copy(ck_hbm.at[window], kold_ref, sem.at[0]),
                pltpu.make_async_copy(cv_hbm.at[window], vold_ref, sem.at[1]))

    def old_step(state):
        left, _ = state
        copies = old_copies(left - 1)
        for cp in copies:
            cp.start()
        for cp in copies:
            cp.wait()
        return left - 1, _sweep_step(cache_chains(kold_ref, vold_ref, 0), acc_ref, c_ref, tri, False)

    lax.while_loop(more, old_step, (n_cache - n_tail, worst))
    for i, (s, h) in enumerate(heads):
        o_ref[s, :, _head_lanes(h)] = acc_ref[i][...].astype(bf16)


def _attn_sample(act, cache_k, cache_v, layer, tri, batch, t_new):
    past = cache_k.shape[2]
    assert t_new <= KEY_BLOCK and past % (KEY_BLOCK * TAIL_BLOCKS) == 0 and batch % SAMPLE_STREAMS == 0
    n_cache = past // KEY_BLOCK
    act3 = act.reshape(batch, t_new, ACT_COLS)
    ck = cache_k.reshape(cache_k.shape[0], batch, past * SB_HEADS, HEAD_DIM)
    cv = cache_v.reshape(cache_v.shape[0], batch, past * SB_HEADS, HEAD_DIM)
    new = pl.BlockSpec((SAMPLE_STREAMS, t_new, SB_WIDTH), lambda g: (g, 0, 0))
    new_q, new_k, new_v = (pl.BlockSpec((SAMPLE_STREAMS, t_new, SB_WIDTH), lambda g, c=c: (g, 0, c))
                           for c in (ACT_Q, ACT_K, ACT_V))
    tail = pl.BlockSpec((None, SAMPLE_STREAMS, TAIL_BLOCKS * CACHE_BLOCK_ROWS, HEAD_DIM),
                        lambda g: (layer, g, n_cache // TAIL_BLOCKS - 1, 0))
    hbm = pl.BlockSpec(memory_space=pl.ANY)
    out = pl.pallas_call(
        functools.partial(_attn_sample_kernel, layer=layer, n_cache=n_cache),
        grid=(batch // SAMPLE_STREAMS,),
        in_specs=[new_q, new_k, new_v, tail, tail, hbm, hbm,
                  pl.BlockSpec((2 * KEY_BLOCK, 2 * KEY_BLOCK), lambda g: (0, 0))],
        out_specs=new,
        out_shape=jax.ShapeDtypeStruct((batch, t_new, SB_WIDTH), bf16),
        scratch_shapes=[
            pltpu.VMEM((SAMPLE_STREAMS, KEY_BLOCK, SB_WIDTH), bf16),
            pltpu.VMEM((SAMPLE_STREAMS, KEY_BLOCK, SB_WIDTH), bf16),
            pltpu.VMEM((SAMPLE_STREAMS, CACHE_BLOCK_ROWS, HEAD_DIM), f32),
            pltpu.VMEM((SAMPLE_STREAMS, CACHE_BLOCK_ROWS, HEAD_DIM), f32),
            pltpu.SemaphoreType.DMA((2,)),
        ] + [pltpu.VMEM((t_new, HEAD_DIM), f32)] * (2 * SAMPLE_STREAMS * SB_HEADS),
        compiler_params=pltpu.CompilerParams(
            dimension_semantics=("arbitrary",), vmem_limit_bytes=VMEM_LIMIT),
        name="attn_sample",
    )(act3, act3, act3, ck, cv, ck, cv, tri)
    return out.reshape(batch * t_new, SB_WIDTH)


def _merge_kernel(a_ref, u_ref, vn_ref, gate_ref, x_ref, ws_ref, bs_ref, wa_ref, wb_ref, wo_ref,
                  h_ref, bb_ref, *, chunk_len):
    tm = a_ref.shape[0]
    rows = lax.broadcasted_iota(jnp.int32, (chunk_len, chunk_len), 0) // CHUNK
    cols = lax.broadcasted_iota(jnp.int32, (chunk_len, chunk_len), 1) // CHUNK
    causal = cols <= rows
    for g in range(SGU_GROUPS):
        w_g = jnp.where(causal, ws_ref[g], 0.0).astype(bf16)
        b_g = bs_ref[g]
        lanes = slice(g * SGU_GROUP_DIM, (g + 1) * SGU_GROUP_DIM)
        for n in range(tm // chunk_len):
            r = slice(n * chunk_len, (n + 1) * chunk_len)
            s = jnp.dot(w_g, vn_ref[r, lanes].astype(bf16), preferred_element_type=f32) + b_g
            bb_ref[r, lanes] = (u_ref[r, lanes].astype(f32) * s).astype(bf16)
    ta = jnp.dot(a_ref[...], wa_ref[...], preferred_element_type=f32)
    tb = jnp.dot(bb_ref[...], wb_ref[...], preferred_element_type=f32)
    merged = (gate_ref[:, :D_MODEL].astype(f32) * ta + gate_ref[:, D_MODEL:].astype(f32) * tb).astype(bf16)
    h_ref[...] = x_ref[...] + jnp.dot(merged, wo_ref[...], preferred_element_type=f32)


def _merge(a, act, vn_all, x, ws, bs, wa, wb, wo, layer, chunk_len, tm):
    m = x.shape[0]
    row = lambda i: (i, 0)
    layer_row = lambda i: (layer * (m // tm) + i, 0)
    const2 = lambda i: (0, 0)
    const3 = lambda i: (0, 0, 0)
    once = pl.Buffered(1)
    assert (ACT_U + 1) * SEG == 2 * D_MODEL
    return pl.pallas_call(
        functools.partial(_merge_kernel, chunk_len=chunk_len),
        grid=(m // tm,),
        in_specs=[
            pl.BlockSpec((tm, SB_WIDTH), row),
            pl.BlockSpec((tm, SGU_WIDTH), lambda i: (i, ACT_U)),
            pl.BlockSpec((tm, SGU_WIDTH), layer_row),
            pl.BlockSpec((tm, 2 * D_MODEL), lambda i: (i, 1)),
            pl.BlockSpec((tm, D_MODEL), row),
            pl.BlockSpec((SGU_GROUPS, chunk_len, chunk_len), const3, pipeline_mode=once),
            pl.BlockSpec((SGU_GROUPS, chunk_len, SGU_GROUP_DIM), const3, pipeline_mode=once),
            pl.BlockSpec((SB_WIDTH, D_MODEL), const2, pipeline_mode=once),
            pl.BlockSpec((SGU_WIDTH, D_MODEL), const2, pipeline_mode=once),
            pl.BlockSpec((D_MODEL, D_MODEL), const2, pipeline_mode=once),
        ],
        out_specs=pl.BlockSpec((tm, D_MODEL), row),
        out_shape=jax.ShapeDtypeStruct((m, D_MODEL), f32),
        scratch_shapes=[pltpu.VMEM((tm, SGU_WIDTH), bf16)],
        compiler_params=pltpu.CompilerParams(
            dimension_semantics=("arbitrary",), vmem_limit_bytes=VMEM_LIMIT),
        name="merge",
    )(a, act, vn_all, act, x, ws, bs, wa, wb, wo)


def _ffn_kernel(*refs, n_rider, final_norm):
    h_ref, g_ref, wg_ref, wu_ref, wd_ref, gf_ref = refs[:6]
    rider_src = refs[6:6 + n_rider]
    o_ref = refs[6 + n_rider]
    rider_dst, hn_ref = refs[7 + n_rider:7 + 2 * n_rider], refs[7 + 2 * n_rider]
    j = pl.program_id(1)
    tm = h_ref.shape[0]
    chunk = min(tm, FFN_ROW_CHUNK)

    def partial_sums(first):
        _run_cast_rider(rider_src, rider_dst)
        for r in range(0, tm, chunk):
            rows = slice(r, r + chunk)
            if first:
                hn_ref[rows, :] = _rmsnorm_rows(h_ref[rows, :], g_ref[...]).astype(bf16)
            hn = hn_ref[rows, :]
            gate = jnp.dot(hn, wg_ref[...], preferred_element_type=f32)
            up = jnp.dot(hn, wu_ref[...], preferred_element_type=f32)
            act = (jax.nn.silu(gate) * up).astype(bf16)
            down = jnp.dot(act, wd_ref[...], preferred_element_type=f32)
            o_ref[rows, :] = (h_ref[rows, :] if first else o_ref[rows, :]) + down

    pl.when(j == 0)(functools.partial(partial_sums, True))
    pl.when(j > 0)(functools.partial(partial_sums, False))

    if final_norm:
        @pl.when(j == pl.num_programs(1) - 1)
        def _():
            o_ref[...] = _rmsnorm_rows(o_ref[...], gf_ref[...])


def _ffn(h, g, w_gate_up, w_down, g_final, to_cast, cast_layer, final_norm, tm, tf):
    m = h.shape[0]
    nm, nf = m // tm, D_FF // tf
    row = lambda i, j: (i, 0)
    vec = pl.BlockSpec((1, D_MODEL), lambda i, j: (0, 0))
    rider_in, rider_out, rider_shapes = _cast_rider(to_cast, cast_layer, nm * nf, lambda i, j: i * nf + j)
    return pl.pallas_call(
        functools.partial(_ffn_kernel, n_rider=len(to_cast), final_norm=final_norm),
        grid=(nm, nf),
        in_specs=[
            pl.BlockSpec((tm, D_MODEL), row),
            vec,
            pl.BlockSpec((D_MODEL, tf), lambda i, j: (0, j)),
            pl.BlockSpec((D_MODEL, tf), lambda i, j: (0, j + nf)),
            pl.BlockSpec((tf, D_MODEL), lambda i, j: (j, 0)),
            vec,
        ] + rider_in,
        out_specs=[pl.BlockSpec((tm, D_MODEL), row)] + rider_out,
        out_shape=[jax.ShapeDtypeStruct((m, D_MODEL), f32)] + rider_shapes,
        scratch_shapes=[pltpu.VMEM((tm, D_MODEL), bf16)],
        compiler_params=pltpu.CompilerParams(
            dimension_semantics=("arbitrary", "arbitrary"), vmem_limit_bytes=VMEM_LIMIT),
        name="ffn",
    )(h, g, w_gate_up, w_gate_up, w_down, g_final, *to_cast)


MERGE_WEIGHTS = ("w_branch_a", "w_branch_b", "w_out")
FFN_WEIGHTS = ("w_gate_up", "w_down")


def _trunk(x, cache_k, cache_v, p, half, tri):
    batch, t = x.shape[0], x.shape[1]
    m = batch * t
    depth = p["w_in"].shape[0]
    chunk_len = min(t, SGU_LEN)
    h = x.reshape(m, D_MODEL)
    stacked = None
    if ("w_in", 0) not in half:
        half["w_in", 0] = p["w_in"][0].astype(bf16)
    for l in range(depth):
        missing = [n for n in MERGE_WEIGHTS if (n, l) not in half]
        act, k_all, v_all, vn_all, *cast = _proj(
            h, p["norm_mix"][l][None], half["w_in", l], p["b_gate"][l][None], p["sgu_norm"][l][None],
            stacked, [p[n] for n in missing], l, depth, tm=min(PROJ_ROWS, m))
        half.update({(n, l): w for n, w in zip(missing, cast)})
        stacked = (k_all, v_all, vn_all)
        missing = [n for n in FFN_WEIGHTS if (n, l) not in half]
        if cache_k is None:
            a, *cast = _attn_prompt(act, tri, [p[n] for n in missing], l, batch, t)
            half.update({(n, l): w for n, w in zip(missing, cast)})
        else:
            a = _attn_sample(act, cache_k, cache_v, l, tri, batch, t)
            half.update({(n, l): p[n][l].astype(bf16) for n in missing})
        bs = jnp.broadcast_to(p["b_spatial"][l][:, :chunk_len, None], (SGU_GROUPS, chunk_len, SGU_GROUP_DIM))
        h = _merge(a, act, vn_all, h, p["w_spatial"][l][:, :chunk_len, :chunk_len], bs,
                   half["w_branch_a", l], half["w_branch_b", l], half["w_out", l], l, chunk_len,
                   tm=min(MERGE_ROWS, m))
        next_in = l + 1 < depth and ("w_in", l + 1) not in half
        h, *cast = _ffn(h, p["norm_ffn"][l][None], half["w_gate_up", l], half["w_down", l], p["norm_final"][None],
                        [p["w_in"]] if next_in else [], l + 1, final_norm=(l == depth - 1),
                        tm=min(FFN_ROWS, m), tf=FFN_COLS)
        if next_in:
            half["w_in", l + 1] = cast[0]
    k_all, v_all, vn_all = stacked
    return (h.reshape(batch, t, D_MODEL), k_all.reshape(depth, batch, t, SB_HEADS, HEAD_DIM),
            v_all.reshape(depth, batch, t, SB_HEADS, HEAD_DIM), vn_all.reshape(depth, batch, t, SGU_WIDTH))


def kernel(x_prompt, x_sample, cache_k, cache_v, norm_mix, w_in, b_gate, sgu_norm, w_spatial, b_spatial,
           w_branch_a, w_branch_b, w_out, norm_ffn, w_gate_up, w_down, norm_final):
    p = dict(
        norm_mix=norm_mix, b_gate=b_gate, sgu_norm=sgu_norm, w_spatial=w_spatial, b_spatial=b_spatial,
        norm_ffn=norm_ffn, norm_final=norm_final, w_in=w_in, w_branch_a=w_branch_a, w_branch_b=w_branch_b,
        w_out=w_out, w_gate_up=w_gate_up, w_down=w_down,
    )
    tri = _suffix_matrix()
    half = {}
    y_prompt, k_prompt, v_prompt, _ = _trunk(x_prompt, None, None, p, half, tri)
    y_sample, k_sample, v_sample, sgu_v_sample = _trunk(x_sample, cache_k, cache_v, p, half, tri)
    return (y_prompt, y_sample, k_prompt, v_prompt, k_sample, v_sample, sgu_v_sample)
```

```python
import functools

import jax
import jax.numpy as jnp
from jax import lax
from jax.experimental import pallas as pl
from jax.experimental.pallas import tpu as pltpu

D_MODEL = 2048
HEAD_DIM = 128
SB_HEADS = 8
SB_WIDTH = SB_HEADS * HEAD_DIM
SGU_GROUPS = 8
SGU_GROUP_DIM = 128
SGU_WIDTH = SGU_GROUPS * SGU_GROUP_DIM
SGU_LEN = 128
CHUNK = 64
D_FF = 5632
EPS = 1e-6
KEY_BLOCK = 128
CACHE_BLOCK_ROWS = KEY_BLOCK * SB_HEADS
TAIL_BLOCKS = 2
FIRST_STEP_TAIL_BLOCKS = 2
SAMPLE_STREAMS = 2
Q_SCALE = HEAD_DIM ** -0.5
WEIGHT_CUTOFF = -105.0
SEG = 1024
PROJ_ROWS = 1024
MERGE_ROWS = 256
FFN_ROWS = 1024
FFN_COLS = 512
FFN_ROW_CHUNK = 512
BF16_SUBLANES = 16
N_PROJ_TILES = (3 * SB_WIDTH + 2 * SGU_WIDTH + 2 * D_MODEL) // SEG
GATE_TILE0 = (3 * SB_WIDTH + 2 * SGU_WIDTH) // SEG
SGU_V_TILE = GATE_TILE0 - 1
ACT_COLS = (N_PROJ_TILES - 1) * SEG
ACT_Q, ACT_K, ACT_V, ACT_U = 0, 1, 2, 3
PROJ_ROW_CHUNK = PROJ_ROWS
VMEM_LIMIT = 56 * 1024 * 1024

bf16 = jnp.bfloat16
f32 = jnp.float32


def _rmsnorm_rows(x, g):
    return x * lax.rsqrt(jnp.mean(x * x, axis=-1, keepdims=True) + EPS) * g


def _store_head_rows(dst_ref, row0, acc):
    n = acc.shape[0]
    for h in range(SB_HEADS):
        dst_ref[pl.ds(row0 * SB_HEADS + h, n, stride=SB_HEADS), :] = acc[:, h * HEAD_DIM:(h + 1) * HEAD_DIM]


def _slab_rows(rows, n_steps):
    for slab in range(BF16_SUBLANES, rows + 1, BF16_SUBLANES):
        if rows % slab == 0 and rows // slab <= n_steps:
            return slab
    raise ValueError(f"no slab size covers {rows} rows in {n_steps} steps")


def _cast_rider(weights, layer, n_steps, step_of):
    in_specs, out_specs, out_shapes = [], [], []
    for w in weights:
        _, rows, cols = w.shape
        slab = _slab_rows(rows, n_steps)
        last = rows // slab - 1
        in_specs.append(pl.BlockSpec(
            (None, slab, cols), lambda *ids, last=last: (layer, jnp.minimum(step_of(*ids), last), 0)))
        out_specs.append(pl.BlockSpec(
            (slab, cols), lambda *ids, last=last: (jnp.minimum(step_of(*ids), last), 0)))
        out_shapes.append(jax.ShapeDtypeStruct((rows, cols), bf16))
    return in_specs, out_specs, out_shapes


def _run_cast_rider(src_refs, dst_refs):
    for src, dst in zip(src_refs, dst_refs):
        dst[...] = src[...].astype(bf16)


def _proj_kernel(*refs, n_alias, n_rider):
    x_hbm, g_ref, w_ref, bg_ref, sn_ref = refs[:5]
    rider_src = refs[5 + n_alias:5 + n_alias + n_rider]
    outs = refs[5 + n_alias + n_rider:]
    act_ref, k_ref, v_ref, vn_ref = outs[:4]
    rider_dst = outs[4:4 + n_rider]
    xn_ref, x_ref, x_sem = outs[4 + n_rider:]
    i, j = pl.program_id(0), pl.program_id(1)
    tm = x_ref.shape[0]
    chunk = min(tm, PROJ_ROW_CHUNK)

    def x_copy(tile):
        return pltpu.make_async_copy(x_hbm.at[pl.ds(tile * tm, tm), :], x_ref, x_sem.at[0])

    def segment(emit, normalise=False):
        _run_cast_rider(rider_src, rider_dst)
        for r in range(0, tm, chunk):
            rows = slice(r, r + chunk)
            if normalise:
                xn_ref[rows, :] = _rmsnorm_rows(x_ref[rows, :], g_ref[...]).astype(bf16)
            emit(r, rows, jnp.dot(xn_ref[rows, :], w_ref[...], preferred_element_type=f32))

    @pl.when(j == 0)
    def _():
        @pl.when(i == 0)
        def _():
            x_copy(0).start()
        x_copy(i).wait()

        def emit(r, rows, acc):
            act_ref[rows, :] = (acc * Q_SCALE).astype(bf16)
        segment(emit, normalise=True)

    @pl.when(j == 1)
    def _():
        @pl.when(i + 1 < pl.num_programs(0))
        def _():
            x_copy(i + 1).start()

        def emit(r, rows, acc):
            act_ref[rows, :] = acc.astype(bf16)
            _store_head_rows(k_ref, r, acc)
        segment(emit)

    @pl.when(j == 2)
    def _():
        def emit(r, rows, acc):
            act_ref[rows, :] = acc.astype(bf16)
            _store_head_rows(v_ref, r, acc)
        segment(emit)

    @pl.when(j == 3)
    def _():
        def emit(r, rows, acc):
            act_ref[rows, :] = jax.nn.gelu(acc).astype(bf16)
        segment(emit)

    @pl.when(j == SGU_V_TILE)
    def _():
        def emit(r, rows, acc):
            vn_ref[rows, :] = _rmsnorm_rows(jax.nn.gelu(acc), sn_ref[...])
        segment(emit)

    @pl.when(j >= GATE_TILE0)
    def _():
        def emit(r, rows, acc):
            act_ref[rows, :] = jax.nn.sigmoid(acc + bg_ref[...]).astype(bf16)
        segment(emit)


def _proj(x, g, w, bg, sn, stacked, to_cast, layer, depth, tm):
    m = x.shape[0]
    nm = m // tm
    act_col = lambda i, j: (i, j - (j >= SGU_V_TILE).astype(jnp.int32))
    row = lambda i, j: (i, 0)
    layer_row = lambda i, j: (layer * nm + i, 0)
    in_specs = [
        pl.BlockSpec(memory_space=pl.ANY),
        pl.BlockSpec((1, D_MODEL), lambda i, j: (0, 0)),
        pl.BlockSpec((D_MODEL, SEG), lambda i, j: (0, j)),
        pl.BlockSpec((1, SEG), lambda i, j: (0, jnp.maximum(j - GATE_TILE0, 0))),
        pl.BlockSpec((1, SEG), lambda i, j: (0, 0)),
    ]
    args = [x, g, w, bg, sn]
    aliases = {}
    if stacked is not None:
        in_specs += [pl.BlockSpec(memory_space=pl.ANY)] * 3
        args += list(stacked)
        aliases = {5: 1, 6: 2, 7: 3}
    rider_in, rider_out, rider_shapes = _cast_rider(
        to_cast, layer, nm * N_PROJ_TILES, lambda i, j: i * N_PROJ_TILES + j)
    return pl.pallas_call(
        functools.partial(_proj_kernel, n_alias=len(args) - 5, n_rider=len(to_cast)),
        grid=(nm, N_PROJ_TILES),
        in_specs=in_specs + rider_in,
        out_specs=[
            pl.BlockSpec((tm, SEG), act_col),
            pl.BlockSpec((tm * SB_HEADS, HEAD_DIM), layer_row),
            pl.BlockSpec((tm * SB_HEADS, HEAD_DIM), layer_row),
            pl.BlockSpec((tm, SEG), layer_row),
        ] + rider_out,
        out_shape=[
            jax.ShapeDtypeStruct((m, ACT_COLS), bf16),
            jax.ShapeDtypeStruct((depth * m * SB_HEADS, HEAD_DIM), f32),
            jax.ShapeDtypeStruct((depth * m * SB_HEADS, HEAD_DIM), f32),
            jax.ShapeDtypeStruct((depth * m, SGU_WIDTH), f32),
        ] + rider_shapes,
        input_output_aliases=aliases,
        scratch_shapes=[pltpu.VMEM((tm, D_MODEL), bf16), pltpu.VMEM((tm, D_MODEL), f32),
                        pltpu.SemaphoreType.DMA((1,))],
        compiler_params=pltpu.CompilerParams(
            dimension_semantics=("arbitrary", "arbitrary"), vmem_limit_bytes=VMEM_LIMIT),
        name="proj",
    )(*args, *to_cast)


def _suffix_matrix():
    i = jnp.arange(KEY_BLOCK)[:, None]
    jj = jnp.arange(KEY_BLOCK)[None, :]
    strict = (i > jj).astype(bf16)
    half = jnp.concatenate([strict, jnp.ones((KEY_BLOCK, KEY_BLOCK), bf16)], axis=1)
    return jnp.concatenate([half, half], axis=0)


def _sweep_step(chains, acc_ref, c_ref, tri, first):
    zs = [lax.dot_general(q, kblk, (((1,), (1,)), ((), ())), preferred_element_type=f32)
          for _, q, kblk, _, _, _ in chains]
    log_betas, sums = [], []
    for z, (_, _, _, _, mask, _) in zip(zs, chains):
        log_beta = jnp.minimum(z, 0.0) - jnp.log(1.0 + jnp.exp(-jnp.abs(z)))
        log_stay = log_beta - z
        if mask is not None:
            log_stay = jnp.where(mask, log_stay, 0.0)
        hi = log_stay.astype(bf16)
        lo = (log_stay - hi.astype(f32)).astype(bf16)
        log_betas.append(log_beta)
        sums.append(jnp.dot(jnp.concatenate([hi, lo], axis=1), tri, preferred_element_type=f32))
    totals, contribs = [], {}
    for log_beta, s, (slot, _, _, vblk, mask, after) in zip(log_betas, sums, chains):
        log_w = log_beta + s[:, :KEY_BLOCK]
        c = s[:, KEY_BLOCK:]
        later = totals[after] if after is not None else (None if first else c_ref[slot][...])
        if later is not None:
            log_w = log_w + later
            c = c + later
        totals.append(c)
        w = jnp.exp(log_w)
        if mask is not None:
            w = jnp.where(mask, w, 0.0)
        contrib = jnp.dot(w.astype(bf16), vblk, preferred_element_type=f32)
        contribs[slot] = contrib if slot not in contribs else contribs[slot] + contrib
    oldest = {slot: c for c, (slot, *_) in zip(totals, chains)}
    worst = None
    for slot, c in oldest.items():
        c_ref[slot][...] = c
        worst = c if worst is None else jnp.maximum(worst, c)
    for slot, contrib in contribs.items():
        if first:
            acc_ref[slot][...] = contrib
        else:
            acc_ref[slot][...] += contrib
    return jnp.max(worst)


def _head_lanes(h):
    return slice(h * HEAD_DIM, (h + 1) * HEAD_DIM)


def _attn_prompt_kernel(*refs, n_rider):
    q_ref, kb_ref, vb_ref, tri_ref = refs[:4]
    rider_src, o_ref = refs[4:4 + n_rider], refs[4 + n_rider]
    rider_dst, state = refs[5 + n_rider:5 + 2 * n_rider], refs[5 + 2 * n_rider:]
    acc_ref, c_ref = state[:SB_HEADS], state[SB_HEADS:]
    qb = pl.program_id(1)
    tri = tri_ref[...]
    rows = lax.broadcasted_iota(jnp.int32, (KEY_BLOCK, KEY_BLOCK), 0)
    cols = lax.broadcasted_iota(jnp.int32, (KEY_BLOCK, KEY_BLOCK), 1)
    causal = cols < rows

    def chains(kblock, mask, after_offset=None):
        ks = pl.multiple_of(kblock * KEY_BLOCK, KEY_BLOCK)
        return [(h, q_ref[:, _head_lanes(h)], kb_ref[pl.ds(ks, KEY_BLOCK), _head_lanes(h)],
                 vb_ref[pl.ds(ks, KEY_BLOCK), _head_lanes(h)], mask,
                 None if after_offset is None else after_offset + h) for h in range(SB_HEADS)]

    def write_out():
        for h in range(SB_HEADS):
            o_ref[:, _head_lanes(h)] = acc_ref[h][...].astype(bf16)

    @pl.when(qb == 0)
    def _():
        _run_cast_rider(rider_src, rider_dst)
        _sweep_step(chains(qb, causal), acc_ref, c_ref, tri, True)
        write_out()

    def sweep_rest(kblock, worst):
        def more(state):
            kblock, worst = state
            return jnp.logical_and(kblock >= 0, worst > WEIGHT_CUTOFF)

        def step(state):
            kblock, _ = state
            return kblock - 1, _sweep_step(chains(kblock, None), acc_ref, c_ref, tri, False)

        lax.while_loop(more, step, (kblock, worst))
        write_out()

    @pl.when(qb == 1)
    def _():
        _run_cast_rider(rider_src, rider_dst)
        _sweep_step(chains(qb, causal) + chains(qb - 1, None, after_offset=0), acc_ref, c_ref, tri, True)
        write_out()

    @pl.when(qb > 1)
    def _():
        _run_cast_rider(rider_src, rider_dst)
        first = (chains(qb, causal) + chains(qb - 1, None, after_offset=0)
                 + chains(qb - 2, None, after_offset=SB_HEADS))
        sweep_rest(qb - 3, _sweep_step(first, acc_ref, c_ref, tri, True))


def _attn_prompt(act, tri, to_cast, layer, batch, seq):
    act3 = act.reshape(batch, seq, ACT_COLS)
    nq = seq // KEY_BLOCK
    qblock = pl.BlockSpec((None, KEY_BLOCK, SB_WIDTH), lambda b, i: (b, i, ACT_Q))
    whole_k = pl.BlockSpec((None, seq, SB_WIDTH), lambda b, i: (b, 0, ACT_K))
    whole_v = pl.BlockSpec((None, seq, SB_WIDTH), lambda b, i: (b, 0, ACT_V))
    rider_in, rider_out, rider_shapes = _cast_rider(to_cast, layer, batch * nq, lambda b, i: b * nq + i)
    out, *cast = pl.pallas_call(
        functools.partial(_attn_prompt_kernel, n_rider=len(to_cast)),
        grid=(batch, nq),
        in_specs=[qblock, whole_k, whole_v,
                  pl.BlockSpec((2 * KEY_BLOCK, 2 * KEY_BLOCK), lambda b, i: (0, 0))] + rider_in,
        out_specs=[pl.BlockSpec((None, KEY_BLOCK, SB_WIDTH), lambda b, i: (b, i, 0))] + rider_out,
        out_shape=[jax.ShapeDtypeStruct((batch, seq, SB_WIDTH), bf16)] + rider_shapes,
        scratch_shapes=[pltpu.VMEM((KEY_BLOCK, HEAD_DIM), f32)] * (2 * SB_HEADS),
        compiler_params=pltpu.CompilerParams(
            dimension_semantics=("arbitrary", "arbitrary"), vmem_limit_bytes=VMEM_LIMIT),
        name="attn_prompt",
    )(act3, act3, act3, tri, *to_cast)
    return (out.reshape(batch * seq, SB_WIDTH), *cast)


def _attn_sample_kernel(q_ref, kn_ref, vn_ref, tk_ref, tv_ref, ck_hbm, cv_hbm, tri_ref, o_ref,
                        kpad_ref, vpad_ref, kold_ref, vold_ref, sem, *state, layer, n_cache):
    g = pl.program_id(0)
    streams, t_new = q_ref.shape[0], q_ref.shape[1]
    acc_ref, c_ref = state[:streams * SB_HEADS], state[streams * SB_HEADS:]
    n_tail = tk_ref.shape[1] // CACHE_BLOCK_ROWS
    tri = tri_ref[...]
    kpad_ref[...] = jnp.zeros(kpad_ref.shape, bf16)
    vpad_ref[...] = jnp.zeros(vpad_ref.shape, bf16)
    kpad_ref[:, 0:t_new, :] = kn_ref[...]
    vpad_ref[:, 0:t_new, :] = vn_ref[...]
    rows = lax.broadcasted_iota(jnp.int32, (t_new, KEY_BLOCK), 0)
    cols = lax.broadcasted_iota(jnp.int32, (t_new, KEY_BLOCK), 1)

    heads = [(s, h) for s in range(streams) for h in range(SB_HEADS)]
    causal = cols < rows
    new_chains = [(i, q_ref[s, :, _head_lanes(h)], kpad_ref[s, :, _head_lanes(h)], vpad_ref[s, :, _head_lanes(h)],
                   causal, None) for i, (s, h) in enumerate(heads)]

    def cache_chains(k_ref, v_ref, first_row, after_offset=None):
        out = []
        for i, (s, h) in enumerate(heads):
            head_rows = pl.ds(first_row + h, KEY_BLOCK, stride=SB_HEADS)
            out.append((i, q_ref[s, :, _head_lanes(h)], k_ref[s, head_rows, :].astype(bf16),
                        v_ref[s, head_rows, :].astype(bf16), None,
                        None if after_offset is None else after_offset + i))
        return out

    first = new_chains
    for t in range(FIRST_STEP_TAIL_BLOCKS):
        first = first + cache_chains(tk_ref, tv_ref, (n_tail - 1 - t) * CACHE_BLOCK_ROWS, after_offset=t * len(heads))
    worst = _sweep_step(first, acc_ref, c_ref, tri, True)

    def more(state):
        left, worst = state
        return jnp.logical_and(left > 0, worst > WEIGHT_CUTOFF)

    def tail_step(state):
        left, _ = state
        first_row = pl.multiple_of((left - 1) * CACHE_BLOCK_ROWS, CACHE_BLOCK_ROWS)
        return left - 1, _sweep_step(cache_chains(tk_ref, tv_ref, first_row), acc_ref, c_ref, tri, False)

    _, worst = lax.while_loop(more, tail_step, (n_tail - FIRST_STEP_TAIL_BLOCKS, worst))

    def old_copies(block):
        window = (layer, pl.ds(g * streams, streams), pl.ds(block * CACHE_BLOCK_ROWS, CACHE_BLOCK_ROWS))
        return (pltpu.make_async_copy(ck_hbm.at[window], kold_ref, sem.at[0]),
                pltpu.make_async_copy(cv_hbm.at[window], vold_ref, sem.at[1]))

    def old_step(state):
        left, _ = state
        copies = old_copies(left - 1)
        for cp in copies:
            cp.start()
        for cp in copies:
            cp.wait()
        return left - 1, _sweep_step(cache_chains(kold_ref, vold_ref, 0), acc_ref, c_ref, tri, False)

    lax.while_loop(more, old_step, (n_cache - n_tail, worst))
    for i, (s, h) in enumerate(heads):
        o_ref[s, :, _head_lanes(h)] = acc_ref[i][...].astype(bf16)


def _attn_sample(act, cache_k, cache_v, layer, tri, batch, t_new):
    past = cache_k.shape[2]
    assert t_new <= KEY_BLOCK and past % (KEY_BLOCK * TAIL_BLOCKS) == 0 and batch % SAMPLE_STREAMS == 0
    n_cache = past // KEY_BLOCK
    act3 = act.reshape(batch, t_new, ACT_COLS)
    ck = cache_k.reshape(cache_k.shape[0], batch, past * SB_HEADS, HEAD_DIM)
    cv = cache_v.reshape(cache_v.shape[0], batch, past * SB_HEADS, HEAD_DIM)
    new = pl.BlockSpec((SAMPLE_STREAMS, t_new, SB_WIDTH), lambda g: (g, 0, 0))
    new_q, new_k, new_v = (pl.BlockSpec((SAMPLE_STREAMS, t_new, SB_WIDTH), lambda g, c=c: (g, 0, c))
                           for c in (ACT_Q, ACT_K, ACT_V))
    tail = pl.BlockSpec((None, SAMPLE_STREAMS, TAIL_BLOCKS * CACHE_BLOCK_ROWS, HEAD_DIM),
                        lambda g: (layer, g, n_cache // TAIL_BLOCKS - 1, 0))
    hbm = pl.BlockSpec(memory_space=pl.ANY)
    out = pl.pallas_call(
        functools.partial(_attn_sample_kernel, layer=layer, n_cache=n_cache),
        grid=(batch // SAMPLE_STREAMS,),
        in_specs=[new_q, new_k, new_v, tail, tail, hbm, hbm,
                  pl.BlockSpec((2 * KEY_BLOCK, 2 * KEY_BLOCK), lambda g: (0, 0))],
        out_specs=new,
        out_shape=jax.ShapeDtypeStruct((batch, t_new, SB_WIDTH), bf16),
        scratch_shapes=[
            pltpu.VMEM((SAMPLE_STREAMS, KEY_BLOCK, SB_WIDTH), bf16),
            pltpu.VMEM((SAMPLE_STREAMS, KEY_BLOCK, SB_WIDTH), bf16),
            pltpu.VMEM((SAMPLE_STREAMS, CACHE_BLOCK_ROWS, HEAD_DIM), f32),
            pltpu.VMEM((SAMPLE_STREAMS, CACHE_BLOCK_ROWS, HEAD_DIM), f32),
            pltpu.SemaphoreType.DMA((2,)),
        ] + [pltpu.VMEM((t_new, HEAD_DIM), f32)] * (2 * SAMPLE_STREAMS * SB_HEADS),
        compiler_params=pltpu.CompilerParams(
            dimension_semantics=("arbitrary",), vmem_limit_bytes=VMEM_LIMIT),
        name="attn_sample",
    )(act3, act3, act3, ck, cv, ck, cv, tri)
    return out.reshape(batch * t_new, SB_WIDTH)


def _merge_kernel(*refs, n_rider, chunk_len):
    a_ref, u_ref, vn_ref, gate_ref, x_ref, ws_ref, bs_ref, wa_ref, wb_ref, wo_ref = refs[:10]
    rider_src, h_ref = refs[10:10 + n_rider], refs[10 + n_rider]
    rider_dst, bb_ref = refs[11 + n_rider:11 + 2 * n_rider], refs[11 + 2 * n_rider]
    _run_cast_rider(rider_src, rider_dst)
    tm = a_ref.shape[0]
    rows = lax.broadcasted_iota(jnp.int32, (chunk_len, chunk_len), 0) // CHUNK
    cols = lax.broadcasted_iota(jnp.int32, (chunk_len, chunk_len), 1) // CHUNK
    causal = cols <= rows
    for g in range(SGU_GROUPS):
        w_g = jnp.where(causal, ws_ref[g], 0.0).astype(bf16)
        b_g = bs_ref[g]
        lanes = slice(g * SGU_GROUP_DIM, (g + 1) * SGU_GROUP_DIM)
        for n in range(tm // chunk_len):
            r = slice(n * chunk_len, (n + 1) * chunk_len)
            s = jnp.dot(w_g, vn_ref[r, lanes].astype(bf16), preferred_element_type=f32) + b_g
            bb_ref[r, lanes] = (u_ref[r, lanes].astype(f32) * s).astype(bf16)
    ta = jnp.dot(a_ref[...], wa_ref[...], preferred_element_type=f32)
    tb = jnp.dot(bb_ref[...], wb_ref[...], preferred_element_type=f32)
    merged = (gate_ref[:, :D_MODEL].astype(f32) * ta + gate_ref[:, D_MODEL:].astype(f32) * tb).astype(bf16)
    h_ref[...] = x_ref[...] + jnp.dot(merged, wo_ref[...], preferred_element_type=f32)


def _merge(a, act, vn_all, x, ws, bs, wa, wb, wo, to_cast, layer, chunk_len, tm):
    m = x.shape[0]
    rider_in, rider_out, rider_shapes = _cast_rider(to_cast, layer, m // tm, lambda i: i)
    row = lambda i: (i, 0)
    layer_row = lambda i: (layer * (m // tm) + i, 0)
    const2 = lambda i: (0, 0)
    const3 = lambda i: (0, 0, 0)
    once = pl.Buffered(1)
    assert (ACT_U + 1) * SEG == 2 * D_MODEL
    return pl.pallas_call(
        functools.partial(_merge_kernel, n_rider=len(to_cast), chunk_len=chunk_len),
        grid=(m // tm,),
        in_specs=[
            pl.BlockSpec((tm, SB_WIDTH), row),
            pl.BlockSpec((tm, SGU_WIDTH), lambda i: (i, ACT_U)),
            pl.BlockSpec((tm, SGU_WIDTH), layer_row),
            pl.BlockSpec((tm, 2 * D_MODEL), lambda i: (i, 1)),
            pl.BlockSpec((tm, D_MODEL), row),
            pl.BlockSpec((SGU_GROUPS, chunk_len, chunk_len), const3, pipeline_mode=once),
            pl.BlockSpec((SGU_GROUPS, chunk_len, SGU_GROUP_DIM), const3, pipeline_mode=once),
            pl.BlockSpec((SB_WIDTH, D_MODEL), const2, pipeline_mode=once),
            pl.BlockSpec((SGU_WIDTH, D_MODEL), const2, pipeline_mode=once),
            pl.BlockSpec((D_MODEL, D_MODEL), const2, pipeline_mode=once),
        ] + rider_in,
        out_specs=[pl.BlockSpec((tm, D_MODEL), row)] + rider_out,
        out_shape=[jax.ShapeDtypeStruct((m, D_MODEL), f32)] + rider_shapes,
        scratch_shapes=[pltpu.VMEM((tm, SGU_WIDTH), bf16)],
        compiler_params=pltpu.CompilerParams(
            dimension_semantics=("arbitrary",), vmem_limit_bytes=VMEM_LIMIT),
        name="merge",
    )(a, act, vn_all, act, x, ws, bs, wa, wb, wo, *to_cast)


def _ffn_kernel(*refs, n_rider, final_norm):
    h_ref, g_ref, wg_ref, wu_ref, wd_ref, gf_ref = refs[:6]
    rider_src = refs[6:6 + n_rider]
    o_ref = refs[6 + n_rider]
    rider_dst, hn_ref = refs[7 + n_rider:7 + 2 * n_rider], refs[7 + 2 * n_rider]
    j = pl.program_id(1)
    tm = h_ref.shape[0]
    chunk = min(tm, FFN_ROW_CHUNK)

    def partial_sums(first):
        _run_cast_rider(rider_src, rider_dst)
        for r in range(0, tm, chunk):
            rows = slice(r, r + chunk)
            if first:
                hn_ref[rows, :] = _rmsnorm_rows(h_ref[rows, :], g_ref[...]).astype(bf16)
            hn = hn_ref[rows, :]
            gate = jnp.dot(hn, wg_ref[...], preferred_element_type=f32)
            up = jnp.dot(hn, wu_ref[...], preferred_element_type=f32)
            act = (jax.nn.silu(gate) * up).astype(bf16)
            down = jnp.dot(act, wd_ref[...], preferred_element_type=f32)
            o_ref[rows, :] = (h_ref[rows, :] if first else o_ref[rows, :]) + down

    pl.when(j == 0)(functools.partial(partial_sums, True))
    pl.when(j > 0)(functools.partial(partial_sums, False))

    if final_norm:
        @pl.when(j == pl.num_programs(1) - 1)
        def _():
            o_ref[...] = _rmsnorm_rows(o_ref[...], gf_ref[...])


def _ffn(h, g, w_gate_up, w_down, g_final, to_cast, cast_layer, final_norm, tm, tf):
    m = h.shape[0]
    nm, nf = m // tm, D_FF // tf
    row = lambda i, j: (i, 0)
    vec = pl.BlockSpec((1, D_MODEL), lambda i, j: (0, 0))
    rider_in, rider_out, rider_shapes = _cast_rider(to_cast, cast_layer, nm * nf, lambda i, j: i * nf + j)
    return pl.pallas_call(
        functools.partial(_ffn_kernel, n_rider=len(to_cast), final_norm=final_norm),
        grid=(nm, nf),
        in_specs=[
            pl.BlockSpec((tm, D_MODEL), row),
            vec,
            pl.BlockSpec((D_MODEL, tf), lambda i, j: (0, j)),
            pl.BlockSpec((D_MODEL, tf), lambda i, j: (0, j + nf)),
            pl.BlockSpec((tf, D_MODEL), lambda i, j: (j, 0)),
            vec,
        ] + rider_in,
        out_specs=[pl.BlockSpec((tm, D_MODEL), row)] + rider_out,
        out_shape=[jax.ShapeDtypeStruct((m, D_MODEL), f32)] + rider_shapes,
        scratch_shapes=[pltpu.VMEM((tm, D_MODEL), bf16)],
        compiler_params=pltpu.CompilerParams(
            dimension_semantics=("arbitrary", "arbitrary"), vmem_limit_bytes=VMEM_LIMIT),
        name="ffn",
    )(h, g, w_gate_up, w_gate_up, w_down, g_final, *to_cast)


MERGE_WEIGHTS = ("w_branch_a", "w_branch_b", "w_out")
FFN_WEIGHTS = ("w_gate_up", "w_down")


def _trunk(x, cache_k, cache_v, p, half, tri):
    batch, t = x.shape[0], x.shape[1]
    m = batch * t
    depth = p["w_in"].shape[0]
    chunk_len = min(t, SGU_LEN)
    h = x.reshape(m, D_MODEL)
    stacked = None
    if ("w_in", 0) not in half:
        half["w_in", 0] = p["w_in"][0].astype(bf16)
    for l in range(depth):
        missing = [n for n in MERGE_WEIGHTS if (n, l) not in half]
        act, k_all, v_all, vn_all, *cast = _proj(
            h, p["norm_mix"][l][None], half["w_in", l], p["b_gate"][l][None], p["sgu_norm"][l][None],
            stacked, [p[n] for n in missing], l, depth, tm=min(PROJ_ROWS, m))
        half.update({(n, l): w for n, w in zip(missing, cast)})
        stacked = (k_all, v_all, vn_all)
        if cache_k is None:
            a, = _attn_prompt(act, tri, [], l, batch, t)
        else:
            a = _attn_sample(act, cache_k, cache_v, l, tri, batch, t)
        missing = [n for n in FFN_WEIGHTS if (n, l) not in half]
        bs = jnp.broadcast_to(p["b_spatial"][l][:, :chunk_len, None], (SGU_GROUPS, chunk_len, SGU_GROUP_DIM))
        h, *cast = _merge(a, act, vn_all, h, p["w_spatial"][l][:, :chunk_len, :chunk_len], bs,
                          half["w_branch_a", l], half["w_branch_b", l], half["w_out", l],
                          [p[n] for n in missing], l, chunk_len, tm=min(MERGE_ROWS, m))
        half.update({(n, l): w for n, w in zip(missing, cast)})
        next_in = l + 1 < depth and ("w_in", l + 1) not in half
        h, *cast = _ffn(h, p["norm_ffn"][l][None], half["w_gate_up", l], half["w_down", l], p["norm_final"][None],
                        [p["w_in"]] if next_in else [], l + 1, final_norm=(l == depth - 1),
                        tm=min(FFN_ROWS, m), tf=FFN_COLS)
        if next_in:
            half["w_in", l + 1] = cast[0]
    k_all, v_all, vn_all = stacked
    return (h.reshape(batch, t, D_MODEL), k_all.reshape(depth, batch, t, SB_HEADS, HEAD_DIM),
            v_all.reshape(depth, batch, t, SB_HEADS, HEAD_DIM), vn_all.reshape(depth, batch, t, SGU_WIDTH))


def kernel(x_prompt, x_sample, cache_k, cache_v, norm_mix, w_in, b_gate, sgu_norm, w_spatial, b_spatial,
           w_branch_a, w_branch_b, w_out, norm_ffn, w_gate_up, w_down, norm_final):
    p = dict(
        norm_mix=norm_mix, b_gate=b_gate, sgu_norm=sgu_norm, w_spatial=w_spatial, b_spatial=b_spatial,
        norm_ffn=norm_ffn, norm_final=norm_final, w_in=w_in, w_branch_a=w_branch_a, w_branch_b=w_branch_b,
        w_out=w_out, w_gate_up=w_gate_up, w_down=w_down,
    )
    tri = _suffix_matrix()
    half = {}
    y_prompt, k_prompt, v_prompt, _ = _trunk(x_prompt, None, None, p, half, tri)
    y_sample, k_sample, v_sample, sgu_v_sample = _trunk(x_sample, cache_k, cache_v, p, half, tri)
    return (y_prompt, y_sample, k_prompt, v_prompt, k_sample, v_sample, sgu_v_sample)
```
